```python
import math
import jax, jax.numpy as jnp
from jax import lax
import numpy as np

D_MODEL = 2048
BATCH = 4
SEQ = 2048
DEPTH = 1
DEC_BATCH = 128
DEC_SEQ = 4
PAST_LEN = 16384
PAGE_SIZE = 128

S5_WIDTH = D_MODEL // 2
S5_GROUP = 16
S5_GROUPS = S5_WIDTH // S5_GROUP
S5_STATE = 64
DT_MIN = 1e-3
DT_MAX = 1e-1
POOL_WIDTH = D_MODEL // 2
POOL_WINDOWS = (2, 4, 8, 16)
POOL_GROUPS = len(POOL_WINDOWS)
POOL_GROUP_WIDTH = POOL_WIDTH // POOL_GROUPS
POOL_BUF = max(POOL_WINDOWS) - 1
IN_WIDTH = S5_WIDTH + POOL_WIDTH + 2 * D_MODEL
PEER_KEYS = 128
PEER_EXPERTS = PEER_KEYS * PEER_KEYS
PEER_HEADS = 8
PEER_QDIM = 256
PEER_HALF = PEER_QDIM // 2
PEER_TOPK = 16
PEER_BLOCK = 128
N_MOD = 6
EPS = 1e-6

kernel_name = "hybrid_s5_pool_peer_adaln_step"


def _rmsnorm(x, g):
    xf = x.astype(jnp.float32)
    return xf * lax.rsqrt(jnp.mean(xf * xf, axis=-1, keepdims=True) + EPS) * g.astype(jnp.float32)


def _s5_discretize(a_re, a_im, log_dt, b_re, b_im):
    a_re = a_re.astype(jnp.float32); a_im = a_im.astype(jnp.float32)
    dt = jnp.exp(log_dt.astype(jnp.float32))[:, None]
    lr, li = a_re * dt, a_im * dt
    mag = jnp.exp(lr)
    ab_re, ab_im = mag * jnp.cos(li), mag * jnp.sin(li)
    num_re, num_im = ab_re - 1.0, ab_im
    den = a_re * a_re + a_im * a_im
    f_re = (num_re * a_re + num_im * a_im) / den
    f_im = (num_im * a_re - num_re * a_im) / den
    b_re = b_re.astype(jnp.float32); b_im = b_im.astype(jnp.float32)
    bb_re = f_re[..., None] * b_re - f_im[..., None] * b_im
    bb_im = f_re[..., None] * b_im + f_im[..., None] * b_re
    return ab_re, ab_im, bb_re, bb_im


def _cplx_combine(e1, e2):
    a1r, a1i, b1r, b1i = e1
    a2r, a2i, b2r, b2i = e2
    return (a2r * a1r - a2i * a1i,
            a2r * a1i + a2i * a1r,
            a2r * b1r - a2i * b1i + b2r,
            a2r * b1i + a2i * b1r + b2i)


def _s5_branch(u, h0, a_re, a_im, log_dt, b_re, b_im, c_re, c_im, d_skip, w_glu, b_glu):
    bsz, t = u.shape[0], u.shape[1]
    ab_re, ab_im, bb_re, bb_im = _s5_discretize(a_re, a_im, log_dt, b_re, b_im)
    ug = u.reshape(bsz, t, S5_GROUPS, S5_GROUP)
    bu_re = jnp.einsum('btgc,gpc->btgp', ug, bb_re)
    bu_im = jnp.einsum('btgc,gpc->btgp', ug, bb_im)
    a_r = jnp.broadcast_to(ab_re, bu_re.shape)
    a_i = jnp.broadcast_to(ab_im, bu_re.shape)
    ap_re, ap_im, h_re, h_im = lax.associative_scan(_cplx_combine, (a_r, a_i, bu_re, bu_im), axis=1)
    if h0 is not None:
        h0_re = h0[0].astype(jnp.float32)[:, None]
        h0_im = h0[1].astype(jnp.float32)[:, None]
        h_re, h_im = (h_re + ap_re * h0_re - ap_im * h0_im,
                      h_im + ap_re * h0_im + ap_im * h0_re)
    y = (jnp.einsum('btgp,gcp->btgc', h_re, c_re.astype(jnp.float32))
         - jnp.einsum('btgp,gcp->btgc', h_im, c_im.astype(jnp.float32))).reshape(bsz, t, S5_WIDTH)
    y = y + d_skip.astype(jnp.float32) * u
    z = jax.nn.gelu(y, approximate=False)
    out = z * jax.nn.sigmoid(z @ w_glu + b_glu)
    return out, h_re[:, -1], h_im[:, -1]


def _pool_branch(u, buf, start_pos, w_pool, pool_scale):
    t = u.shape[1]
    ext = jnp.concatenate([buf.astype(jnp.float32), u], axis=1)
    cs = jnp.concatenate([jnp.zeros_like(ext[:, :1]), jnp.cumsum(ext, axis=1)], axis=1)
    end = cs[:, POOL_BUF + 1:]
    pos = start_pos + jnp.arange(t)
    outs = []
    for g, w in enumerate(POOL_WINDOWS):
        sl = slice(g * POOL_GROUP_WIDTH, (g + 1) * POOL_GROUP_WIDTH)
        cnt = jnp.minimum(pos + 1, w).astype(jnp.float32)[None, :, None]
        start = cs[:, POOL_BUF + 1 - w:POOL_BUF + 1 - w + t, sl]
        pooled = (end[..., sl] - start) / cnt - u[..., sl]
        outs.append(pooled @ w_pool[g])
    y = jnp.concatenate(outs, axis=-1) * pool_scale
    return y, ext[:, -POOL_BUF:]


def _peer(x, w_q, keys, u_tab, v_tab):
    bsz, t, d = x.shape
    xt = x.reshape(-1, d)
    n = xt.shape[0]
    xt = jnp.pad(xt, ((0, (-n) % PEER_BLOCK), (0, 0)))
    q = (xt @ w_q).reshape(-1, PEER_HEADS, 2, PEER_HALF)
    s = jnp.einsum('nhkd,hkmd->nhkm', q, keys).astype(jnp.float32)
    top_s, top_i = lax.top_k(s, PEER_TOPK)
    comb = (top_s[:, :, 0, :, None] + top_s[:, :, 1, None, :]).reshape(-1, PEER_HEADS, PEER_TOPK * PEER_TOPK)
    cidx = (top_i[:, :, 0, :, None] * PEER_KEYS + top_i[:, :, 1, None, :]).reshape(-1, PEER_HEADS, PEER_TOPK * PEER_TOPK)
    best_s, best_j = lax.top_k(comb, PEER_TOPK)
    expert = jnp.take_along_axis(cidx, best_j, axis=-1)
    gate = jax.nn.softmax(best_s, axis=-1)
    nb = xt.shape[0] // PEER_BLOCK
    xb = xt.reshape(nb, PEER_BLOCK, d)
    eb = expert.reshape(nb, PEER_BLOCK, PEER_HEADS * PEER_TOPK)
    gb = gate.reshape(nb, PEER_BLOCK, PEER_HEADS * PEER_TOPK)

    def block(args):
        xc, ec, gc = args
        act = jax.nn.gelu(jnp.einsum('nd,nkd->nk', xc, u_tab[ec]).astype(jnp.float32), approximate=False)
        return jnp.einsum('nk,nkd->nd', gc * act, v_tab[ec])

    out = lax.map(block, (xb, eb, gb)).reshape(-1, d)[:n]
    return out.reshape(bsz, t, d)


def _layer(x, c, h0, buf, start_pos, lw):
    (w_ada, b_ada, g1, g2, w_in, a_re, a_im, log_dt, b_re, b_im, c_re, c_im, d_skip,
     w_glu, b_glu, w_pool, pool_scale, p_a, p_b, w_o, w_q, keys, u_tab, v_tab) = lw
    mod = jax.nn.silu(c.astype(jnp.float32)) @ w_ada + b_ada
    sh1, sc1, gt1, sh2, sc2, gt2 = jnp.split(mod[:, None, :], N_MOD, axis=-1)
    h = _rmsnorm(x, g1) * (1.0 + sc1) + sh1
    z = h @ w_in
    u_a, u_b, z_ga, z_gb = jnp.split(z, [S5_WIDTH, S5_WIDTH + POOL_WIDTH, S5_WIDTH + POOL_WIDTH + D_MODEL], axis=-1)
    y_a, s_re, s_im = _s5_branch(u_a, h0, a_re, a_im, log_dt, b_re, b_im, c_re, c_im, d_skip, w_glu, b_glu)
    if buf is None:
        buf = jnp.zeros((u_b.shape[0], POOL_BUF, POOL_WIDTH), jnp.float32)
    y_b, new_buf = _pool_branch(u_b, buf, start_pos, w_pool, pool_scale)
    merged = jax.nn.sigmoid(z_ga) * (y_a @ p_a) + jax.nn.sigmoid(z_gb) * (y_b @ p_b)
    x = x + gt1 * (merged @ w_o)
    h2 = _rmsnorm(x, g2) * (1.0 + sc2) + sh2
    x = x + gt2 * _peer(h2, w_q, keys, u_tab, v_tab)
    return x, s_re, s_im, new_buf


def setup_inputs(seed: int = 0) -> dict:
    key = jax.random.key(seed)
    ks = iter(jax.random.split(key, 40))
    f32 = jnp.float32
    nrm = lambda shape, s: jax.random.normal(next(ks), shape, f32) * s
    inp = {}
    inp['x_prompt'] = nrm((BATCH, SEQ, D_MODEL), 1.0)
    inp['x_sample'] = nrm((DEC_BATCH, DEC_SEQ, D_MODEL), 1.0)
    inp['state_s5_re'] = nrm((DEPTH, DEC_BATCH, S5_GROUPS, S5_STATE), 0.5)
    inp['state_s5_im'] = nrm((DEPTH, DEC_BATCH, S5_GROUPS, S5_STATE), 0.5)
    inp['state_pool'] = nrm((DEPTH, DEC_BATCH, POOL_BUF, POOL_WIDTH), 1.0)
    inp['c_prompt'] = nrm((BATCH, D_MODEL), 1.0)
    inp['c_sample'] = nrm((DEC_BATCH, D_MODEL), 1.0)
    inp['w_ada'] = nrm((DEPTH, D_MODEL, N_MOD * D_MODEL), D_MODEL ** -0.5)
    inp['b_ada'] = nrm((DEPTH, N_MOD * D_MODEL), 0.01)
    inp['g_norm1'] = 1.0 + nrm((DEPTH, D_MODEL), 0.01)
    inp['g_norm2'] = 1.0 + nrm((DEPTH, D_MODEL), 0.01)
    inp['w_in'] = nrm((DEPTH, D_MODEL, IN_WIDTH), D_MODEL ** -0.5)
    inp['s5_a_re'] = -0.5 + nrm((DEPTH, S5_GROUPS, S5_STATE), 0.01)
    inp['s5_a_im'] = (math.pi * jnp.arange(S5_STATE, dtype=f32))[None, None, :] + nrm((DEPTH, S5_GROUPS, S5_STATE), 0.01)
    inp['s5_log_dt'] = jax.random.uniform(next(ks), (DEPTH, S5_GROUPS), f32, math.log(DT_MIN), math.log(DT_MAX))
    inp['s5_b_re'] = nrm((DEPTH, S5_GROUPS, S5_STATE, S5_GROUP), (2 * S5_GROUP) ** -0.5)
    inp['s5_b_im'] = nrm((DEPTH, S5_GROUPS, S5_STATE, S5_GROUP), (2 * S5_GROUP) ** -0.5)
    inp['s5_c_re'] = nrm((DEPTH, S5_GROUPS, S5_GROUP, S5_STATE), (2 * S5_STATE) ** -0.5)
    inp['s5_c_im'] = nrm((DEPTH, S5_GROUPS, S5_GROUP, S5_STATE), (2 * S5_STATE) ** -0.5)
    inp['s5_d'] = nrm((DEPTH, S5_WIDTH), 1.0)
    inp['w_glu'] = nrm((DEPTH, S5_WIDTH, S5_WIDTH), S5_WIDTH ** -0.5)
    inp['b_glu'] = nrm((DEPTH, S5_WIDTH), 0.01)
    inp['w_pool'] = nrm((DEPTH, POOL_GROUPS, POOL_GROUP_WIDTH, POOL_GROUP_WIDTH), POOL_GROUP_WIDTH ** -0.5)
    inp['pool_scale'] = 1.0 + nrm((DEPTH, POOL_WIDTH), 0.02)
    inp['w_proj_a'] = nrm((DEPTH, S5_WIDTH, D_MODEL), S5_WIDTH ** -0.5)
    inp['w_proj_b'] = nrm((DEPTH, POOL_WIDTH, D_MODEL), POOL_WIDTH ** -0.5)
    inp['w_out'] = nrm((DEPTH, D_MODEL, D_MODEL), D_MODEL ** -0.5)
    inp['peer_wq'] = nrm((DEPTH, D_MODEL, PEER_HEADS * PEER_QDIM), D_MODEL ** -0.5)
    inp['peer_keys'] = nrm((DEPTH, PEER_HEADS, 2, PEER_KEYS, PEER_HALF), PEER_HALF ** -0.5)
    inp['peer_u'] = nrm((DEPTH, PEER_EXPERTS, D_MODEL), D_MODEL ** -0.5)
    inp['peer_v'] = nrm((DEPTH, PEER_EXPERTS, D_MODEL), PEER_HEADS ** -0.5)
    inp['g_final'] = 1.0 + nrm((D_MODEL,), 0.01)
    return inp


def reference(x_prompt, x_sample, state_s5_re, state_s5_im, state_pool, c_prompt, c_sample,
              w_ada, b_ada, g_norm1, g_norm2, w_in, s5_a_re, s5_a_im, s5_log_dt, s5_b_re, s5_b_im,
              s5_c_re, s5_c_im, s5_d, w_glu, b_glu, w_pool, pool_scale, w_proj_a, w_proj_b, w_out,
              peer_wq, peer_keys, peer_u, peer_v, g_final):
    yp, ys = x_prompt, x_sample
    p_re, p_im, p_buf, s_re, s_im, s_buf = [], [], [], [], [], []
    for l in range(DEPTH):
        lw = (w_ada[l], b_ada[l], g_norm1[l], g_norm2[l], w_in[l], s5_a_re[l], s5_a_im[l], s5_log_dt[l],
              s5_b_re[l], s5_b_im[l], s5_c_re[l], s5_c_im[l], s5_d[l], w_glu[l], b_glu[l], w_pool[l],
              pool_scale[l], w_proj_a[l], w_proj_b[l], w_out[l], peer_wq[l], peer_keys[l], peer_u[l], peer_v[l])
        yp, r_re, r_im, r_buf = _layer(yp, c_prompt, None, None, 0, lw)
        ys, q_re, q_im, q_buf = _layer(ys, c_sample, (state_s5_re[l], state_s5_im[l]), state_pool[l], PAST_LEN, lw)
        p_re.append(r_re); p_im.append(r_im); p_buf.append(r_buf)
        s_re.append(q_re); s_im.append(q_im); s_buf.append(q_buf)
    y_prompt = _rmsnorm(yp, g_final).astype(x_prompt.dtype)
    y_sample = _rmsnorm(ys, g_final).astype(x_sample.dtype)
    return (y_prompt, y_sample,
            jnp.stack(p_re), jnp.stack(p_im), jnp.stack(p_buf),
            jnp.stack(s_re), jnp.stack(s_im), jnp.stack(s_buf))
```

```python
import functools
import math

import numpy as np
import jax
import jax.numpy as jnp
from jax import lax
from jax.experimental import pallas as pl
from jax.experimental.pallas import tpu as pltpu

F32 = jnp.float32
BF16 = jnp.bfloat16
EPS = 1e-6
PAST_LEN = 16384
POOL_WINDOWS = (2, 4, 8, 16)
PEER_TOPK = 16
S5_CHUNK = 16
LANES = 128
VMEM_LIMIT = 56 * 1024 * 1024


def _gelu(x):
    return 0.5 * x * (1.0 + lax.erf(x * (1.0 / math.sqrt(2.0))))


def _rms(x, g):
    ms = jnp.mean(x * x, axis=-1, keepdims=True)
    return x * lax.rsqrt(ms + EPS) * g


def _params(*sem):
    return pltpu.CompilerParams(dimension_semantics=sem, vmem_limit_bytes=VMEM_LIMIT)


def _mod_spec(arr, tiles_per_group):
    _, r, d = arr.shape
    return pl.BlockSpec((1, r, d), lambda i, *_: (i // tiles_per_group, 0, 0))


def _ada_body(c_ref, w_ref, b_ref, o_ref):
    c = c_ref[...]
    s = (c * jax.nn.sigmoid(c)).astype(BF16)
    o_ref[...] = jnp.dot(s, w_ref[...].astype(BF16), preferred_element_type=F32) + b_ref[...]


def _ada(c, w, b, tn=1024):
    m, d = c.shape
    n = w.shape[1]
    return pl.pallas_call(
        _ada_body,
        grid=(n // tn,),
        in_specs=[pl.BlockSpec((m, d), lambda j: (0, 0)),
                  pl.BlockSpec((d, tn), lambda j: (0, j)),
                  pl.BlockSpec((1, tn), lambda j: (0, j))],
        out_specs=pl.BlockSpec((m, tn), lambda j: (0, j)),
        out_shape=jax.ShapeDtypeStruct((m, n), F32),
        compiler_params=_params("parallel"),
        name="ada",
    )(c, w, b.reshape(1, n))


def _in_proj_body(x_ref, sc_ref, sh_ref, g_ref, w_ref, u_ref, gate_ref, h_scr, *, n_u):
    n = pl.program_id(1)

    @pl.when(n == 0)
    def _():
        h = _rms(x_ref[...], g_ref[...]) * (1.0 + sc_ref[0]) + sh_ref[0]
        h_scr[...] = h.astype(BF16)

    acc = jnp.dot(h_scr[...], w_ref[...], preferred_element_type=F32)

    @pl.when(n < n_u)
    def _():
        u_ref[...] = acc

    @pl.when(n >= n_u)
    def _():
        gate_ref[...] = jax.nn.sigmoid(acc).astype(BF16)


def _in_proj(x, sc, sh, g1, w_bf, n_lin, tm, tiles_per_group):
    n_tok, d = x.shape
    n_all = w_bf.shape[1]
    tn = n_lin // 2
    n_u = n_lin // tn
    return pl.pallas_call(
        functools.partial(_in_proj_body, n_u=n_u),
        grid=(n_tok // tm, n_all // tn),
        in_specs=[pl.BlockSpec((tm, d), lambda i, n: (i, 0)),
                  _mod_spec(sc, tiles_per_group), _mod_spec(sh, tiles_per_group),
                  pl.BlockSpec((1, d), lambda i, n: (0, 0)),
                  pl.BlockSpec((d, tn), lambda i, n: (0, n))],
        out_specs=[pl.BlockSpec((tm, tn), lambda i, n: (i, jnp.minimum(n, n_u - 1))),
                   pl.BlockSpec((tm, tn), lambda i, n: (i, jnp.maximum(n - n_u, 0)))],
        out_shape=[jax.ShapeDtypeStruct((n_tok, n_lin), F32),
                   jax.ShapeDtypeStruct((n_tok, n_all - n_lin), BF16)],
        scratch_shapes=[pltpu.VMEM((tm, d), BF16)],
        compiler_params=_params("parallel", "arbitrary"),
        name="in_proj",
    )(x, sc, sh, g1.reshape(1, d), w_bf)


def _s5_prep_body(are_ref, aim_ref, ldt_ref, btre_ref, btim_ref, cre_ref, cim_ref,
                  k_ref, ere_ref, eim_ref, fre_ref, fim_ref, pwre_ref, pwim_ref, *, gb, n_lag, n_lvl, small_pow):
    hi = lax.Precision.HIGHEST
    nt = (((1,), (1,)), ((), ()))
    for j in range(gb):
        a_re, a_im = are_ref[j], aim_ref[j]
        dt = jnp.exp(ldt_ref[j])
        lr, li = a_re * dt, a_im * dt
        mag = jnp.exp(lr)
        ab_re, ab_im = mag * jnp.cos(li), mag * jnp.sin(li)
        num_re, num_im = ab_re - 1.0, ab_im
        den = a_re * a_re + a_im * a_im
        f_re = (num_re * a_re + num_im * a_im) / den
        f_im = (num_im * a_re - num_re * a_im) / den
        bt_re, bt_im = btre_ref[j], btim_ref[j]
        bb_re = f_re * bt_re - f_im * bt_im
        bb_im = f_re * bt_im + f_im * bt_re
        c_re, c_im = cre_ref[j], cim_ref[j]
        pows = [(jnp.ones_like(ab_re), jnp.zeros_like(ab_re))]
        for _ in range(n_lag):
            p_re, p_im = pows[-1]
            pows.append((p_re * ab_re - p_im * ab_im, p_re * ab_im + p_im * ab_re))
        g_re, g_im = [], []
        for k in range(n_lag):
            p_re, p_im = pows[k]
            ere_ref[j, k] = p_re * bb_re - p_im * bb_im
            eim_ref[j, k] = p_re * bb_im + p_im * bb_re
            g_re.append(c_re * p_re - c_im * p_im)
            g_im.append(c_re * p_im + c_im * p_re)
            q_re, q_im = pows[k + 1]
            fre_ref[j, k] = c_re * q_re - c_im * q_im
            fim_ref[j, k] = c_re * q_im + c_im * q_re
        gr = jnp.concatenate(g_re, axis=0)
        gi = jnp.concatenate(g_im, axis=0)
        k_ref[j] = (lax.dot_general(gr, bb_re, nt, precision=hi, preferred_element_type=F32)
                    - lax.dot_general(gi, bb_im, nt, precision=hi, preferred_element_type=F32))
        p_re, p_im = pows[n_lag]
        for lvl in range(n_lvl):
            pwre_ref[j, lvl:lvl + 1, :] = p_re
            pwim_ref[j, lvl:lvl + 1, :] = p_im
            p_re, p_im = p_re * p_re - p_im * p_im, 2.0 * p_re * p_im
        q_re, q_im = pows[small_pow]
        pwre_ref[j, n_lvl:n_lvl + 1, :] = q_re
        pwim_ref[j, n_lvl:n_lvl + 1, :] = q_im


def _s5_prep(a_re, a_im, log_dt, b_re, b_im, c_re, c_im, n_lag, n_lvl, small_pow, gb=8):
    g, p = a_re.shape
    c = b_re.shape[-1]
    row = lambda a: a.reshape(g, 1, -1)
    bt = lambda a: jnp.swapaxes(a, 1, 2)
    vec = lambda w: pl.BlockSpec((gb, 1, w), lambda i: (i, 0, 0))
    mat = pl.BlockSpec((gb, c, p), lambda i: (i, 0, 0))
    lag = pl.BlockSpec((gb, n_lag, c, p), lambda i: (i, 0, 0, 0))
    lag_shape = jax.ShapeDtypeStruct((g, n_lag, c, p), F32)
    pw_shape = jax.ShapeDtypeStruct((g, n_lvl + 1, p), F32)
    pw = pl.BlockSpec((gb, n_lvl + 1, p), lambda i: (i, 0, 0))
    return pl.pallas_call(
        functools.partial(_s5_prep_body, gb=gb, n_lag=n_lag, n_lvl=n_lvl, small_pow=small_pow),
        grid=(g // gb,),
        in_specs=[vec(p), vec(p), vec(1), mat, mat, mat, mat],
        out_specs=[pl.BlockSpec((gb, n_lag * c, c), lambda i: (i, 0, 0)), lag, lag, lag, lag, pw, pw],
        out_shape=[jax.ShapeDtypeStruct((g, n_lag * c, c), F32), lag_shape, lag_shape, lag_shape, lag_shape,
                   pw_shape, pw_shape],
        compiler_params=_params("parallel"),
        name="s5_prep",
    )(row(a_re), row(a_im), row(log_dt), bt(b_re), bt(b_im), c_re, c_im)


def _s5_matrices(k_out, e_re, e_im, f_re, f_im, d_skip, length):
    g, n_lag, c, p = e_re.shape
    kk = k_out.reshape(g, n_lag, c, c)
    s_idx = np.arange(length)[:, None]
    t_idx = np.arange(length)[None, :]
    lag = t_idx - s_idx
    toep = kk[:, np.clip(lag, 0, None)] * jnp.asarray(lag >= 0, F32)[None, :, :, None, None]
    toep = toep.transpose(0, 1, 4, 2, 3).reshape(g, length * c, length * c)
    take = np.arange(length - 1, -1, -1)
    si = jnp.concatenate([e_re[:, take], e_im[:, take]], axis=-1).reshape(g, length * c, 2 * p)
    so_re = f_re[:, :length].transpose(0, 3, 1, 2).reshape(g, p, length * c)
    so_im = f_im[:, :length].transpose(0, 3, 1, 2).reshape(g, p, length * c)
    so = jnp.concatenate([so_re, -so_im], axis=1)
    d_tile = jnp.tile(d_skip.reshape(g, 1, 1, c), (1, 1, length, 1)).reshape(g, 1, length * c)
    return toep.astype(BF16), si.astype(BF16), so.astype(BF16), d_tile


def _s5_body(x_ref, m_ref, si_ref, so_ref, ar_ref, ai_ref, d_ref, z_ref, hf_ref, *, gb, n_seq, n_chunk, p):
    rows = n_seq * n_chunk
    ridx = lax.broadcasted_iota(jnp.int32, (rows, 2 * p), 0) % n_chunk
    for j in range(gb):
        x = x_ref[j]
        xb = x.astype(BF16)
        h = jnp.dot(xb, si_ref[j], preferred_element_type=F32)
        d, lvl = 1, 0
        while d < n_chunk:
            sh = jnp.where(ridx >= d, pltpu.roll(h, d, 0), 0.0)
            sw = pltpu.roll(sh, p, 1)
            h = h + ar_ref[j, lvl:lvl + 1, :] * sh + ai_ref[j, lvl:lvl + 1, :] * sw
            d, lvl = 2 * d, lvl + 1
        hs = jnp.where(ridx >= 1, pltpu.roll(h, 1, 0), 0.0)
        y = (jnp.dot(xb, m_ref[j], preferred_element_type=F32)
             + jnp.dot(hs.astype(BF16), so_ref[j], preferred_element_type=F32))
        z_ref[j] = _gelu(y + d_ref[j] * x)
        for b in range(n_seq):
            hf_ref[j, b:b + 1, :] = h[(b + 1) * n_chunk - 1:(b + 1) * n_chunk, :]


def _s5_prompt(xg, m, si, so, ar, ai, d_tile, n_seq, gb=4):
    g, rows, w = xg.shape
    p2 = si.shape[-1]
    n_chunk = rows // n_seq
    blk = lambda a: pl.BlockSpec((gb,) + a.shape[1:], lambda i: (i,) + (0,) * (a.ndim - 1))
    return pl.pallas_call(
        functools.partial(_s5_body, gb=gb, n_seq=n_seq, n_chunk=n_chunk, p=p2 // 2),
        grid=(g // gb,),
        in_specs=[blk(xg), blk(m), blk(si), blk(so), blk(ar), blk(ai), blk(d_tile)],
        out_specs=[pl.BlockSpec((gb, rows, w), lambda i: (i, 0, 0)),
                   pl.BlockSpec((gb, n_seq, p2), lambda i: (i, 0, 0))],
        out_shape=[jax.ShapeDtypeStruct((g, rows, w), F32), jax.ShapeDtypeStruct((g, n_seq, p2), F32)],
        compiler_params=_params("parallel"),
        name="s5_prompt",
    )(xg, m, si, so, ar, ai, d_tile)


def _s5_sample_body(x_ref, h0_ref, m_ref, si_ref, so_ref, ar_ref, ai_ref, d_ref, z_ref, hn_ref, *, gb, p, row):
    for j in range(gb):
        x = x_ref[j]
        h0 = h0_ref[j]
        xb = x.astype(BF16)
        y = (jnp.dot(xb, m_ref[j], preferred_element_type=F32)
             + jnp.dot(h0.astype(BF16), so_ref[j], preferred_element_type=F32))
        z_ref[j] = _gelu(y + d_ref[j] * x)
        hn_ref[j] = (ar_ref[j, row:row + 1, :] * h0 + ai_ref[j, row:row + 1, :] * pltpu.roll(h0, p, 1)
                     + jnp.dot(xb, si_ref[j], preferred_element_type=F32))


def _s5_sample(xg, h0, m, si, so, ar, ai, d_tile, row, gb=8):
    g, nb, w = xg.shape
    p2 = h0.shape[-1]
    blk = lambda a: pl.BlockSpec((gb,) + a.shape[1:], lambda i: (i,) + (0,) * (a.ndim - 1))
    return pl.pallas_call(
        functools.partial(_s5_sample_body, gb=gb, p=p2 // 2, row=row),
        grid=(g // gb,),
        in_specs=[blk(xg), blk(h0), blk(m), blk(si), blk(so), blk(ar), blk(ai), blk(d_tile)],
        out_specs=[pl.BlockSpec((gb, nb, w), lambda i: (i, 0, 0)), pl.BlockSpec((gb, nb, p2), lambda i: (i, 0, 0))],
        out_shape=[jax.ShapeDtypeStruct((g, nb, w), F32), jax.ShapeDtypeStruct((g, nb, p2), F32)],
        compiler_params=_params("parallel"),
        name="s5_sample",
    )(xg, h0, m, si, so, ar, ai, d_tile)


def _mixer_body(z_ref, ub_ref, halo_ref, ga_ref, gb_ref, wglu_ref, bglu_ref, wpool_ref, ps_ref, pa_ref, pb_ref,
                o_ref, ext_scr, *, tm, stride, halo_rows, tiles_per_seq, count_positions):
    i = pl.program_id(0)
    z = z_ref[...]
    glu = z * jax.nn.sigmoid(jnp.dot(z.astype(BF16), wglu_ref[...], preferred_element_type=F32) + bglu_ref[...])
    a = jnp.dot(glu.astype(BF16), pa_ref[...], preferred_element_type=F32)

    ext_scr[0:halo_rows, :] = halo_ref[...]
    if count_positions:
        @pl.when(i % tiles_per_seq == 0)
        def _():
            ext_scr[0:halo_rows, :] = jnp.zeros((halo_rows, ext_scr.shape[1]), F32)
    ext_scr[halo_rows:halo_rows + tm, :] = ub_ref[...]

    gw = wpool_ref.shape[1]
    b = jnp.zeros(o_ref.shape, F32)
    for gi, win in enumerate(POOL_WINDOWS):
        lanes = slice(gi * gw, (gi + 1) * gw)
        u = ext_scr[halo_rows:halo_rows + tm, lanes]
        acc = u
        for k in range(1, win):
            lo = halo_rows - k * stride
            acc = acc + ext_scr[lo:lo + tm, lanes]
        if count_positions:
            pos = (i % tiles_per_seq) * tm + lax.broadcasted_iota(jnp.int32, (tm, gw), 0)
            cnt = jnp.minimum(pos + 1, win).astype(F32)
        else:
            cnt = float(win)
        pooled = acc / cnt - u
        yb = jnp.dot(pooled.astype(BF16), wpool_ref[gi], preferred_element_type=F32) * ps_ref[:, lanes]
        b = b + jnp.dot(yb.astype(BF16), pb_ref[lanes, :], preferred_element_type=F32)
    o_ref[...] = (ga_ref[...].astype(F32) * a + gb_ref[...].astype(F32) * b).astype(BF16)


def _mixer(z, u, halo, gates, w_glu, b_glu, w_pool, pool_scale, p_a, p_b, tm, stride, tiles_per_seq,
           count_positions):
    n_tok, wa = z.shape
    d = p_a.shape[1]
    halo_rows = 16 * stride
    if halo is None:
        per = tm // halo_rows
        halo_arr = u
        halo_spec = pl.BlockSpec((halo_rows, wa), lambda i: (jnp.maximum(i * per - 1, 0), 1))
    else:
        halo_arr = halo
        halo_spec = pl.BlockSpec(halo.shape, lambda i: (0, 0))
    full = lambda a: pl.BlockSpec(a.shape, lambda i: (0,) * a.ndim)
    return pl.pallas_call(
        functools.partial(_mixer_body, tm=tm, stride=stride, halo_rows=halo_rows, tiles_per_seq=tiles_per_seq,
                          count_positions=count_positions),
        grid=(n_tok // tm,),
        in_specs=[pl.BlockSpec((tm, wa), lambda i: (i, 0)),
                  pl.BlockSpec((tm, wa), lambda i: (i, 1)),
                  halo_spec,
                  pl.BlockSpec((tm, d), lambda i: (i, 0)),
                  pl.BlockSpec((tm, d), lambda i: (i, 1)),
                  full(w_glu), pl.BlockSpec((1, wa), lambda i: (0, 0)), full(w_pool),
                  pl.BlockSpec((1, wa), lambda i: (0, 0)), full(p_a), full(p_b)],
        out_specs=pl.BlockSpec((tm, d), lambda i: (i, 0)),
        out_shape=jax.ShapeDtypeStruct((n_tok, d), BF16),
        scratch_shapes=[pltpu.VMEM((halo_rows + tm, wa), F32)],
        compiler_params=_params("arbitrary"),
        name="mixer",
    )(z, u, halo_arr, gates, gates, w_glu, b_glu.reshape(1, wa), w_pool, pool_scale.reshape(1, wa), p_a, p_b)


def _out_proj_body(m_ref, x_ref, gt_ref, sc_ref, sh_ref, g_ref, wo_ref, x1_ref, h2t_ref):
    y = jnp.dot(m_ref[...], wo_ref[...], preferred_element_type=F32)
    x1 = x_ref[...] + gt_ref[0] * y
    x1_ref[...] = x1
    h2 = _rms(x1, g_ref[...]) * (1.0 + sc_ref[0]) + sh_ref[0]
    h2t_ref[...] = h2.T.astype(BF16)


def _out_proj(merged, x, gt, sc, sh, g2, w_o, tm, tiles_per_group):
    n_tok, d = x.shape
    return pl.pallas_call(
        _out_proj_body,
        grid=(n_tok // tm,),
        in_specs=[pl.BlockSpec((tm, d), lambda i: (i, 0)), pl.BlockSpec((tm, d), lambda i: (i, 0)),
                  _mod_spec(gt, tiles_per_group), _mod_spec(sc, tiles_per_group), _mod_spec(sh, tiles_per_group),
                  pl.BlockSpec((1, d), lambda i: (0, 0)), pl.BlockSpec((d, d), lambda i: (0, 0))],
        out_specs=[pl.BlockSpec((tm, d), lambda i: (i, 0)), pl.BlockSpec((d, tm), lambda i: (0, i))],
        out_shape=[jax.ShapeDtypeStruct((n_tok, d), F32), jax.ShapeDtypeStruct((d, n_tok), BF16)],
        compiler_params=_params("parallel"),
        name="out_proj",
    )(merged, x, gt, sc, sh, g2.reshape(1, d), w_o)


def _peer_candidates(topk):
    return [(a, b) for a in range(topk) for b in range(topk) if (a + 1) * (b + 1) <= topk]


def _router_body(h2t_ref, wqt_ref, keys_ref, r1_ref, e1_ref, cnt_ref, e0_ref,
                 q_scr, s_scr, rk_scr, t_scr, cn_scr, zi_scr, *, tm, n_head, n_key, half, topk):
    n_lc = tm // LANES
    q_scr[...] = jnp.dot(wqt_ref[...], h2t_ref[...], preferred_element_type=F32).astype(BF16)
    for hk in range(2 * n_head):
        h, k = hk % n_head, hk // n_head
        off = (h * 2 + k) * half
        s_scr[hk] = jnp.dot(keys_ref[hk], q_scr[off:off + half, :], preferred_element_type=F32)

    key_idx = lax.broadcasted_iota(jnp.int32, (n_key, LANES), 0).astype(F32)

    for hk in range(2 * n_head):
        for lc in range(n_lc):
            lanes = slice(lc * LANES, (lc + 1) * LANES)

            def one_round(a, sr, hk=hk, lanes=lanes):
                s, rk = sr
                m = jnp.max(s, axis=0, keepdims=True)
                first = jnp.min(jnp.where(s == m, key_idx, float(n_key)), axis=0, keepdims=True)
                sel = key_idx == first
                t_scr[a, hk:hk + 1, lanes] = m
                return jnp.where(sel, -jnp.inf, s), jnp.where(sel, a.astype(F32), rk)

            _, rk = lax.fori_loop(0, topk, one_round,
                                  (s_scr[hk, :, lanes], jnp.full((n_key, LANES), float(topk), F32)))
            rk_scr[hk, :, lanes] = rk

    cands = _peer_candidates(topk)
    for lc in range(n_lc):
        lanes = slice(lc * LANES, (lc + 1) * LANES)
        t0 = [t_scr[a, 0:n_head, lanes] for a in range(topk)]
        t1 = [t_scr[b, n_head:2 * n_head, lanes] for b in range(topk)]
        val = [t0[a] + t1[b] for a, b in cands]
        rank = []
        for a, b in cands:
            rank.append(jnp.full((n_head, LANES), float((a + 1) * (b + 1) - 1), F32))
        for c1, (a1, b1) in enumerate(cands):
            for c2 in range(c1 + 1, len(cands)):
                a2, b2 = cands[c2]
                if (a1 <= a2 and b1 <= b2) or (a2 <= a1 and b2 <= b1):
                    continue
                ge = jnp.where(val[c1] >= val[c2], 1.0, 0.0)
                rank[c2] = rank[c2] + ge
                rank[c1] = rank[c1] + (1.0 - ge)
        sel = [jnp.where(r < float(topk), 1.0, 0.0) for r in rank]
        zsum = jnp.zeros((n_head, LANES), F32)
        cnt = [jnp.zeros((n_head, LANES), F32) for _ in range(topk)]
        for c, (a, b) in enumerate(cands):
            zsum = zsum + sel[c] * jnp.exp(val[c] - val[0])
            cnt[a] = cnt[a] + sel[c]
        zi_scr[:, lanes] = 1.0 / zsum
        for a in range(topk):
            cn_scr[a, :, lanes] = cnt[a]

    for h in range(n_head):
        r0 = rk_scr[h]
        c = jnp.zeros((n_key, tm), F32)
        for a in range(topk):
            c = jnp.where(r0 == float(a), cn_scr[a, h:h + 1, :], c)
        cnt_ref[h] = c
        e0_ref[h] = jnp.exp(s_scr[h] - t_scr[0, h:h + 1, :]) * zi_scr[h:h + 1, :]
        e1_ref[h] = jnp.exp(s_scr[n_head + h] - t_scr[0, n_head + h:n_head + h + 1, :]).astype(BF16)
        r1_ref[h] = rk_scr[n_head + h].astype(BF16)


def _router(h2t, wq_t, keys_hk, n_head, tm):
    d, n_tok = h2t.shape
    _, n_key, half = keys_hk.shape
    topk = PEER_TOPK
    out = lambda dt: jax.ShapeDtypeStruct((n_head, n_key, n_tok), dt)
    ospec = pl.BlockSpec((n_head, n_key, tm), lambda i: (0, 0, i))
    return pl.pallas_call(
        functools.partial(_router_body, tm=tm, n_head=n_head, n_key=n_key, half=half, topk=topk),
        grid=(n_tok // tm,),
        in_specs=[pl.BlockSpec((d, tm), lambda i: (0, i)),
                  pl.BlockSpec(wq_t.shape, lambda i: (0, 0)),
                  pl.BlockSpec(keys_hk.shape, lambda i: (0, 0, 0))],
        out_specs=[ospec, ospec, ospec, ospec],
        out_shape=[out(BF16), out(BF16), out(F32), out(F32)],
        scratch_shapes=[pltpu.VMEM((wq_t.shape[0], tm), BF16),
                        pltpu.VMEM((2 * n_head, n_key, tm), F32),
                        pltpu.VMEM((2 * n_head, n_key, tm), F32),
                        pltpu.VMEM((topk, 2 * n_head, tm), F32),
                        pltpu.VMEM((topk, n_head, tm), F32),
                        pltpu.VMEM((n_head, tm), F32)],
        compiler_params=_params("parallel"),
        name="router",
    )(h2t, wq_t, keys_hk)


def _peer_body(h2t_ref, u_ref, v_ref, r1_ref, e1_ref, cnt_ref, e0_ref, o_ref, g_scr, *, ni, n_key, n_head, tm):
    e = pl.program_id(1)

    @pl.when(e == 0)
    def _():
        o_ref[...] = jnp.zeros(o_ref.shape, F32)

    act = jnp.dot(u_ref[...], h2t_ref[...], preferred_element_type=F32)
    base = pl.multiple_of(e * ni, ni)
    cblk = [cnt_ref[h, pl.ds(base, ni), :] for h in range(n_head)]
    eblk = [e0_ref[h, pl.ds(base, ni), :] for h in range(n_head)]
    for ii in range(ni):
        w = jnp.zeros((n_key, tm), BF16)
        for h in range(n_head):
            cb = jnp.broadcast_to(cblk[h][ii:ii + 1, :], (n_key, tm)).astype(BF16)
            eb = jnp.broadcast_to(eblk[h][ii:ii + 1, :], (n_key, tm)).astype(BF16)
            w = w + jnp.where(r1_ref[h] < cb, e1_ref[h], jnp.zeros((), BF16)) * eb
        g_scr[ii * n_key:(ii + 1) * n_key, :] = _gelu(act[ii * n_key:(ii + 1) * n_key, :]).astype(BF16) * w
    o_ref[...] += lax.dot_general(g_scr[...], v_ref[...], (((0,), (0,)), ((), ())), preferred_element_type=F32)


def _peer(h2t, u_bf, v_bf, r1, e1, cnt, e0, tm, ni):
    d, n_tok = h2t.shape
    n_head, n_key, _ = r1.shape
    te = ni * n_key
    hspec = pl.BlockSpec((n_head, n_key, tm), lambda t, e: (0, 0, t))
    return pl.pallas_call(
        functools.partial(_peer_body, ni=ni, n_key=n_key, n_head=n_head, tm=tm),
        grid=(n_tok // tm, n_key // ni),
        in_specs=[pl.BlockSpec((d, tm), lambda t, e: (0, t)),
                  pl.BlockSpec((te, d), lambda t, e: (e, 0)),
                  pl.BlockSpec((te, d), lambda t, e: (e, 0)),
                  hspec, hspec, hspec, hspec],
        out_specs=pl.BlockSpec((tm, d), lambda t, e: (t, 0)),
        out_shape=jax.ShapeDtypeStruct((n_tok, d), F32),
        scratch_shapes=[pltpu.VMEM((te, tm), BF16)],
        compiler_params=_params("parallel", "arbitrary"),
        name="peer",
    )(h2t, u_bf, v_bf, r1, e1, cnt, e0)


def _final_body(x1_ref, p_ref, gt_ref, gf_ref, o_ref, *, last):
    x2 = x1_ref[...] + gt_ref[0] * p_ref[...]
    o_ref[...] = _rms(x2, gf_ref[...]) if last else x2


def _final(x1, peer_out, row_off, gt, g_final, tm, tiles_per_group, last):
    n_tok, d = x1.shape
    off = row_off // tm
    return pl.pallas_call(
        functools.partial(_final_body, last=last),
        grid=(n_tok // tm,),
        in_specs=[pl.BlockSpec((tm, d), lambda i: (i, 0)), pl.BlockSpec((tm, d), lambda i: (i + off, 0)),
                  _mod_spec(gt, tiles_per_group), pl.BlockSpec((1, d), lambda i: (0, 0))],
        out_specs=pl.BlockSpec((tm, d), lambda i: (i, 0)),
        out_shape=jax.ShapeDtypeStruct((n_tok, d), F32),
        compiler_params=_params("parallel"),
        name="final",
    )(x1, peer_out, gt, g_final.reshape(1, d))


def kernel(x_prompt, x_sample, state_s5_re, state_s5_im, state_pool, c_prompt, c_sample, w_ada, b_ada, g_norm1, g_norm2, w_in, s5_a_re, s5_a_im, s5_log_dt, s5_b_re, s5_b_im, s5_c_re, s5_c_im, s5_d, w_glu, b_glu, w_pool, pool_scale, w_proj_a, w_proj_b, w_out, peer_wq, peer_keys, peer_u, peer_v, g_final):
    depth = w_ada.shape[0]
    nb, nt, d = x_prompt.shape
    sb, st, _ = x_sample.shape
    n_grp, n_state, grp_w = s5_b_re.shape[1:]
    s5_w = n_grp * grp_w
    n_head = peer_keys.shape[1]
    n_key = peer_keys.shape[3]
    pool_buf = state_pool.shape[2]
    n_p, n_s = nb * nt, sb * st
    n_chunk = nt // S5_CHUNK
    n_lvl = int(math.log2(n_chunk))
    assert 2 ** n_lvl == n_chunk and st <= S5_CHUNK and pool_buf == max(POOL_WINDOWS) - 1

    tm_a, tm_b, tm_r, tm_e = 512, 256, 256, 512
    assert nt % tm_a == 0 and n_s % tm_a == 0 and (n_p + n_s) % tm_e == 0

    xp = x_prompt.reshape(n_p, d)
    xs = jnp.swapaxes(x_sample, 0, 1).reshape(n_s, d)
    c_all = jnp.concatenate([c_prompt, c_sample], axis=0)
    c_all = jnp.pad(c_all, ((0, (-c_all.shape[0]) % 8), (0, 0)))

    new_p = {"re": [], "im": [], "buf": []}
    new_s = {"re": [], "im": [], "buf": []}
    for l in range(depth):
        last = l == depth - 1
        mod = _ada(c_all, w_ada[l], b_ada[l])
        mods_p = [m.reshape(nb, 1, d) for m in jnp.split(mod[:nb], 6, axis=-1)]
        mods_s_rows = [jnp.tile(m, (st, 1)) for m in jnp.split(mod[nb:nb + sb], 6, axis=-1)]
        mods_s = lambda tm: [m.reshape(n_s // tm, tm, d) for m in mods_s_rows]

        w_in_bf = w_in[l].astype(BF16)
        u_p, gates_p = _in_proj(xp, mods_p[1], mods_p[0], g_norm1[l], w_in_bf, 2 * s5_w, tm_a, nt // tm_a)
        ms = mods_s(tm_a)
        u_s, gates_s = _in_proj(xs, ms[1], ms[0], g_norm1[l], w_in_bf, 2 * s5_w, tm_a, 1)

        k_out, e_re, e_im, f_re, f_im, pw_re, pw_im = _s5_prep(
            s5_a_re[l], s5_a_im[l], s5_log_dt[l], s5_b_re[l], s5_b_im[l], s5_c_re[l], s5_c_im[l],
            S5_CHUNK, n_lvl, st)
        ar = jnp.concatenate([pw_re, pw_re], axis=-1)
        ai = jnp.concatenate([-pw_im, pw_im], axis=-1)
        m16, si16, so16, d16 = _s5_matrices(k_out, e_re, e_im, f_re, f_im, s5_d[l], S5_CHUNK)
        m4, si4, so4, d4 = _s5_matrices(k_out, e_re, e_im, f_re, f_im, s5_d[l], st)

        xg_p = (u_p[:, :s5_w].reshape(nb * n_chunk, S5_CHUNK, n_grp, grp_w)
                .transpose(2, 0, 1, 3).reshape(n_grp, nb * n_chunk, S5_CHUNK * grp_w))
        zg_p, hf_p = _s5_prompt(xg_p, m16, si16, so16, ar, ai, d16, nb)
        z_p = (zg_p.reshape(n_grp, nb * n_chunk, S5_CHUNK, grp_w).transpose(1, 2, 0, 3).reshape(n_p, s5_w))
        hf_p = hf_p.transpose(1, 0, 2)
        new_p["re"].append(hf_p[..., :n_state])
        new_p["im"].append(hf_p[..., n_state:])

        xg_s = (u_s[:, :s5_w].reshape(st, sb, n_grp, grp_w).transpose(2, 1, 0, 3).reshape(n_grp, sb, st * grp_w))
        h0 = jnp.concatenate([state_s5_re[l], state_s5_im[l]], axis=-1).transpose(1, 0, 2)
        zg_s, hn_s = _s5_sample(xg_s, h0, m4, si4, so4, ar, ai, d4, n_lvl)
        z_s = zg_s.reshape(n_grp, sb, st, grp_w).transpose(2, 1, 0, 3).reshape(n_s, s5_w)
        hn_s = hn_s.transpose(1, 0, 2)
        new_s["re"].append(hn_s[..., :n_state])
        new_s["im"].append(hn_s[..., n_state:])

        new_p["buf"].append(u_p.reshape(nb, nt, 2 * s5_w)[:, nt - pool_buf:, s5_w:])
        hist = jnp.swapaxes(state_pool[l], 0, 1)
        ub_s = u_s[:, s5_w:].reshape(st, sb, s5_w)
        new_s["buf"].append(jnp.swapaxes(jnp.concatenate([hist, ub_s], axis=0)[st:], 0, 1))
        halo_s = jnp.concatenate([jnp.zeros((1, sb, s5_w), F32), hist], axis=0).reshape((pool_buf + 1) * sb, s5_w)

        wb = lambda a: a.astype(BF16)
        mix_w = (wb(w_glu[l]), b_glu[l], wb(w_pool[l]), pool_scale[l], wb(w_proj_a[l]), wb(w_proj_b[l]))
        merged_p = _mixer(z_p, u_p, None, gates_p, *mix_w, tm=tm_b, stride=1, tiles_per_seq=nt // tm_b,
                          count_positions=True)
        merged_s = _mixer(z_s, u_s, halo_s, gates_s, *mix_w, tm=n_s, stride=sb, tiles_per_seq=1,
                          count_positions=False)

        w_o_bf = wb(w_out[l])
        x1_p, h2t_p = _out_proj(merged_p, xp, mods_p[2], mods_p[4], mods_p[3], g_norm2[l], w_o_bf, tm_b, nt // tm_b)
        ms = mods_s(tm_b)
        x1_s, h2t_s = _out_proj(merged_s, xs, ms[2], ms[4], ms[3], g_norm2[l], w_o_bf, tm_b, 1)

        h2t = jnp.concatenate([h2t_p, h2t_s], axis=1)
        keys_hk = wb(jnp.swapaxes(peer_keys[l], 0, 1).reshape(2 * n_head, n_key, -1))
        r1, e1, cnt, e0 = _router(h2t, wb(peer_wq[l].T), keys_hk, n_head, tm_r)
        peer_out = _peer(h2t, wb(peer_u[l]), wb(peer_v[l]), r1, e1, cnt, e0, tm_e, 8)

        xp = _final(x1_p, peer_out, 0, mods_p[5], g_final, tm_a, nt // tm_a, last)
        xs = _final(x1_s, peer_out, n_p, mods_s(tm_a)[5], g_final, tm_a, 1, last)

    y_prompt = xp.reshape(nb, nt, d)
    y_sample = jnp.swapaxes(xs.reshape(st, sb, d), 0, 1)
    stack = lambda xs_: jnp.stack(xs_)
    return (y_prompt, y_sample, stack(new_p["re"]), stack(new_p["im"]), stack(new_p["buf"]),
            stack(new_s["re"]), stack(new_s["im"]), stack(new_s["buf"]))
```

```python
import functools
import math

import numpy as np
import jax
import jax.numpy as jnp
from jax import lax
from jax.experimental import pallas as pl
from jax.experimental.pallas import tpu as pltpu

F32 = jnp.float32
BF16 = jnp.bfloat16
EPS = 1e-6
PAST_LEN = 16384
POOL_WINDOWS = (2, 4, 8, 16)
PEER_TOPK = 16
S5_CHUNK = 16
LANES = 128
VMEM_LIMIT = 56 * 1024 * 1024


def _gelu(x):
    return 0.5 * x * (1.0 + lax.erf(x * (1.0 / math.sqrt(2.0))))


def _rms(x, g):
    ms = jnp.mean(x * x, axis=-1, keepdims=True)
    return x * lax.rsqrt(ms + EPS) * g


def _params(*sem):
    return pltpu.CompilerParams(dimension_semantics=sem, vmem_limit_bytes=VMEM_LIMIT)


def _mod_spec(arr, tiles_per_group):
    _, r, d = arr.shape
    return pl.BlockSpec((1, r, d), lambda i, *_: (i // tiles_per_group, 0, 0))


def _ada_body(c_ref, w_ref, b_ref, o_ref):
    c = c_ref[...]
    s = (c * jax.nn.sigmoid(c)).astype(BF16)
    o_ref[...] = jnp.dot(s, w_ref[...].astype(BF16), preferred_element_type=F32) + b_ref[...]


def _ada(c, w, b, tn=1024):
    m, d = c.shape
    n = w.shape[1]
    return pl.pallas_call(
        _ada_body,
        grid=(n // tn,),
        in_specs=[pl.BlockSpec((m, d), lambda j: (0, 0)),
                  pl.BlockSpec((d, tn), lambda j: (0, j)),
                  pl.BlockSpec((1, tn), lambda j: (0, j))],
        out_specs=pl.BlockSpec((m, tn), lambda j: (0, j)),
        out_shape=jax.ShapeDtypeStruct((m, n), F32),
        compiler_params=_params("parallel"),
        name="ada",
    )(c, w, b.reshape(1, n))


def _in_proj_body(x_ref, sc_ref, sh_ref, g_ref, w_ref, u_ref, gate_ref, h_scr, *, n_u, n_sub):
    n = pl.program_id(1)

    @pl.when(n == 0)
    def _():
        h = _rms(x_ref[...], g_ref[...]) * (1.0 + sc_ref[0]) + sh_ref[0]
        h_scr[...] = h.astype(BF16)

    tn = w_ref.shape[1]
    cols = [slice(c, c + n_sub) for c in range(0, tn, n_sub)]

    @pl.when(n < n_u)
    def _():
        for c in cols:
            u_ref[:, c] = jnp.dot(h_scr[...], w_ref[:, c], preferred_element_type=F32)

    @pl.when(n >= n_u)
    def _():
        for c in cols:
            acc = jnp.dot(h_scr[...], w_ref[:, c], preferred_element_type=F32)
            gate_ref[:, c] = jax.nn.sigmoid(acc).astype(BF16)


def _in_proj(x, sc, sh, g1, w_bf, n_lin, tm, tiles_per_group):
    n_tok, d = x.shape
    n_all = w_bf.shape[1]
    tn = n_lin // 2
    n_u = n_lin // tn
    return pl.pallas_call(
        functools.partial(_in_proj_body, n_u=n_u, n_sub=256),
        grid=(n_tok // tm, n_all // tn),
        in_specs=[pl.BlockSpec((tm, d), lambda i, n: (i, 0)),
                  _mod_spec(sc, tiles_per_group), _mod_spec(sh, tiles_per_group),
                  pl.BlockSpec((1, d), lambda i, n: (0, 0)),
                  pl.BlockSpec((d, tn), lambda i, n: (0, n))],
        out_specs=[pl.BlockSpec((tm, tn), lambda i, n: (i, jnp.minimum(n, n_u - 1))),
                   pl.BlockSpec((tm, tn), lambda i, n: (i, jnp.maximum(n - n_u, 0)))],
        out_shape=[jax.ShapeDtypeStruct((n_tok, n_lin), F32),
                   jax.ShapeDtypeStruct((n_tok, n_all - n_lin), BF16)],
        scratch_shapes=[pltpu.VMEM((tm, d), BF16)],
        compiler_params=_params("parallel", "arbitrary"),
        name="in_proj",
    )(x, sc, sh, g1.reshape(1, d), w_bf)


def _s5_prep_body(are_ref, aim_ref, ldt_ref, btre_ref, btim_ref, cre_ref, cim_ref,
                  k_ref, ere_ref, eim_ref, fre_ref, fim_ref, pwre_ref, pwim_ref, *, gb, n_lag, n_lvl, small_pow):
    hi = lax.Precision.HIGHEST
    nt = (((1,), (1,)), ((), ()))
    for j in range(gb):
        a_re, a_im = are_ref[j], aim_ref[j]
        dt = jnp.exp(ldt_ref[j])
        lr, li = a_re * dt, a_im * dt
        mag = jnp.exp(lr)
        ab_re, ab_im = mag * jnp.cos(li), mag * jnp.sin(li)
        num_re, num_im = ab_re - 1.0, ab_im
        den = a_re * a_re + a_im * a_im
        f_re = (num_re * a_re + num_im * a_im) / den
        f_im = (num_im * a_re - num_re * a_im) / den
        bt_re, bt_im = btre_ref[j], btim_ref[j]
        bb_re = f_re * bt_re - f_im * bt_im
        bb_im = f_re * bt_im + f_im * bt_re
        c_re, c_im = cre_ref[j], cim_ref[j]
        pows = [(jnp.ones_like(ab_re), jnp.zeros_like(ab_re))]
        for _ in range(n_lag):
            p_re, p_im = pows[-1]
            pows.append((p_re * ab_re - p_im * ab_im, p_re * ab_im + p_im * ab_re))
        g_re, g_im = [], []
        for k in range(n_lag):
            p_re, p_im = pows[k]
            ere_ref[j, k] = p_re * bb_re - p_im * bb_im
            eim_ref[j, k] = p_re * bb_im + p_im * bb_re
            g_re.append(c_re * p_re - c_im * p_im)
            g_im.append(c_re * p_im + c_im * p_re)
            q_re, q_im = pows[k + 1]
            fre_ref[j, k] = c_re * q_re - c_im * q_im
            fim_ref[j, k] = c_re * q_im + c_im * q_re
        gr = jnp.concatenate(g_re, axis=0)
        gi = jnp.concatenate(g_im, axis=0)
        k_ref[j] = (lax.dot_general(gr, bb_re, nt, precision=hi, preferred_element_type=F32)
                    - lax.dot_general(gi, bb_im, nt, precision=hi, preferred_element_type=F32))
        p_re, p_im = pows[n_lag]
        for lvl in range(n_lvl):
            pwre_ref[j, lvl:lvl + 1, :] = p_re
            pwim_ref[j, lvl:lvl + 1, :] = p_im
            p_re, p_im = p_re * p_re - p_im * p_im, 2.0 * p_re * p_im
        q_re, q_im = pows[small_pow]
        pwre_ref[j, n_lvl:n_lvl + 1, :] = q_re
        pwim_ref[j, n_lvl:n_lvl + 1, :] = q_im


def _s5_prep(a_re, a_im, log_dt, b_re, b_im, c_re, c_im, n_lag, n_lvl, small_pow, gb=8):
    g, p = a_re.shape
    c = b_re.shape[-1]
    row = lambda a: a.reshape(g, 1, -1)
    bt = lambda a: jnp.swapaxes(a, 1, 2)
    vec = lambda w: pl.BlockSpec((gb, 1, w), lambda i: (i, 0, 0))
    mat = pl.BlockSpec((gb, c, p), lambda i: (i, 0, 0))
    lag = pl.BlockSpec((gb, n_lag, c, p), lambda i: (i, 0, 0, 0))
    lag_shape = jax.ShapeDtypeStruct((g, n_lag, c, p), F32)
    pw_shape = jax.ShapeDtypeStruct((g, n_lvl + 1, p), F32)
    pw = pl.BlockSpec((gb, n_lvl + 1, p), lambda i: (i, 0, 0))
    return pl.pallas_call(
        functools.partial(_s5_prep_body, gb=gb, n_lag=n_lag, n_lvl=n_lvl, small_pow=small_pow),
        grid=(g // gb,),
        in_specs=[vec(p), vec(p), vec(1), mat, mat, mat, mat],
        out_specs=[pl.BlockSpec((gb, n_lag * c, c), lambda i: (i, 0, 0)), lag, lag, lag, lag, pw, pw],
        out_shape=[jax.ShapeDtypeStruct((g, n_lag * c, c), F32), lag_shape, lag_shape, lag_shape, lag_shape,
                   pw_shape, pw_shape],
        compiler_params=_params("parallel"),
        name="s5_prep",
    )(row(a_re), row(a_im), row(log_dt), bt(b_re), bt(b_im), c_re, c_im)


def _s5_matrices(k_out, e_re, e_im, f_re, f_im, length):
    g, n_lag, c, p = e_re.shape
    pad = -(length * c) % LANES
    kk = k_out.reshape(g, n_lag, c, c)
    s_idx = np.arange(length)[:, None]
    t_idx = np.arange(length)[None, :]
    lag = t_idx - s_idx
    toep = kk[:, np.clip(lag, 0, None)] * jnp.asarray(lag >= 0, F32)[None, :, :, None, None]
    toep = toep.transpose(0, 1, 4, 2, 3).reshape(g, length * c, length * c)
    take = np.arange(length - 1, -1, -1)
    si = jnp.concatenate([e_re[:, take], e_im[:, take]], axis=-1).reshape(g, length * c, 2 * p)
    so_re = f_re[:, :length].transpose(0, 3, 1, 2).reshape(g, p, length * c)
    so_im = f_im[:, :length].transpose(0, 3, 1, 2).reshape(g, p, length * c)
    so = jnp.concatenate([so_re, -so_im], axis=1)
    toep = jnp.pad(toep, ((0, 0), (0, pad), (0, pad)))
    si = jnp.pad(si, ((0, 0), (0, pad), (0, 0)))
    so = jnp.pad(so, ((0, 0), (0, 0), (0, pad)))
    return toep.astype(BF16), si.astype(BF16), so.astype(BF16)


def _to_chunk_rows(u_t, n_grp, cw):
    per = LANES // cw
    blk = lax.broadcasted_iota(jnp.int32, u_t[0].shape, 1) // cw
    out = []
    for g in range(n_grp):
        cols = []
        for t0 in range(0, len(u_t), per):
            acc = None
            for tt, u in enumerate(u_t[t0:t0 + per]):
                shift = ((tt - g) * cw) % LANES
                piece = pltpu.roll(u, shift, 1) if shift else u
                acc = piece if acc is None else jnp.where(blk == tt, piece, acc)
            cols.append(acc)
        out.append(cols[0] if len(cols) == 1 else jnp.concatenate(cols, axis=1))
    return out


def _to_token_rows(y_g, n_t, cw):
    per = LANES // cw
    blk = lax.broadcasted_iota(jnp.int32, (y_g[0].shape[0], LANES), 1) // cw
    out = []
    for t in range(n_t):
        col, tt = t // per, t % per
        acc = None
        for g, y in enumerate(y_g):
            src = y[:, col * LANES:(col + 1) * LANES]
            shift = ((g - tt) * cw) % LANES
            piece = pltpu.roll(src, shift, 1) if shift else src
            acc = piece if acc is None else jnp.where(blk == g, piece, acc)
        out.append(acc)
    return out


def _s5_prompt_body(u_ref, m_ref, si_ref, so_ref, ar_ref, ai_ref, d_ref, z_ref, hf_ref,
                    *, n_grp, cw, n_t, n_seq, n_chunk, p):
    rows = n_seq * n_chunk
    u_t = [u_ref[pl.ds(t, rows, stride=n_t), :] for t in range(n_t)]
    x_g = _to_chunk_rows(u_t, n_grp, cw)
    ridx = lax.broadcasted_iota(jnp.int32, (rows, 2 * p), 0) % n_chunk
    y_g = []
    for j in range(n_grp):
        xb = x_g[j].astype(BF16)
        h = jnp.dot(xb, si_ref[j], preferred_element_type=F32)
        d, lvl = 1, 0
        while d < n_chunk:
            sh = jnp.where(ridx >= d, pltpu.roll(h, d, 0), 0.0)
            sw = pltpu.roll(sh, p, 1)
            h = h + ar_ref[j, lvl:lvl + 1, :] * sh + ai_ref[j, lvl:lvl + 1, :] * sw
            d, lvl = 2 * d, lvl + 1
        hs = jnp.where(ridx >= 1, pltpu.roll(h, 1, 0), 0.0)
        y_g.append(jnp.dot(xb, m_ref[j], preferred_element_type=F32)
                   + jnp.dot(hs.astype(BF16), so_ref[j], preferred_element_type=F32))
        for b in range(n_seq):
            hf_ref[j, b:b + 1, :] = h[(b + 1) * n_chunk - 1:(b + 1) * n_chunk, :]
    for t, y in enumerate(_to_token_rows(y_g, n_t, cw)):
        z_ref[pl.ds(t, rows, stride=n_t), :] = _gelu(y + d_ref[...] * u_t[t])


def _s5_prompt(u, m, si, so, ar, ai, d_skip, n_seq, n_t, cw):
    n_tok = u.shape[0]
    g, _, p2 = si.shape
    n_grp = LANES // cw
    n_chunk = n_tok // (n_seq * n_t)
    wblk = lambda a: pl.BlockSpec((n_grp,) + a.shape[1:], lambda i: (i,) + (0,) * (a.ndim - 1))
    return pl.pallas_call(
        functools.partial(_s5_prompt_body, n_grp=n_grp, cw=cw, n_t=n_t, n_seq=n_seq, n_chunk=n_chunk, p=p2 // 2),
        grid=(g // n_grp,),
        in_specs=[pl.BlockSpec((n_tok, LANES), lambda i: (0, i)),
                  wblk(m), wblk(si), wblk(so), wblk(ar), wblk(ai),
                  pl.BlockSpec((1, LANES), lambda i: (0, i))],
        out_specs=[pl.BlockSpec((n_tok, LANES), lambda i: (0, i)),
                   pl.BlockSpec((n_grp, n_seq, p2), lambda i: (i, 0, 0))],
        out_shape=[jax.ShapeDtypeStruct((n_tok, g * cw), F32), jax.ShapeDtypeStruct((g, n_seq, p2), F32)],
        compiler_params=_params("parallel"),
        name="s5_prompt",
    )(u, m, si, so, ar, ai, d_skip.reshape(1, g * cw))


def _s5_sample_body(u_ref, h0_ref, m_ref, si_ref, so_ref, ar_ref, ai_ref, d_ref, z_ref, hn_ref,
                    *, n_grp, cw, n_t, n_b, p, row):
    u_t = [u_ref[t * n_b:(t + 1) * n_b, :] for t in range(n_t)]
    x_g = _to_chunk_rows(u_t, n_grp, cw)
    y_g = []
    for j in range(n_grp):
        xb = x_g[j].astype(BF16)
        h0 = h0_ref[j]
        y_g.append(jnp.dot(xb, m_ref[j], preferred_element_type=F32)
                   + jnp.dot(h0.astype(BF16), so_ref[j], preferred_element_type=F32))
        hn_ref[j] = (ar_ref[j, row:row + 1, :] * h0 + ai_ref[j, row:row + 1, :] * pltpu.roll(h0, p, 1)
                     + jnp.dot(xb, si_ref[j], preferred_element_type=F32))
    for t, y in enumerate(_to_token_rows(y_g, n_t, cw)):
        z_ref[t * n_b:(t + 1) * n_b, :] = _gelu(y + d_ref[...] * u_t[t])


def _s5_sample(u, h0, m, si, so, ar, ai, d_skip, row, n_t, cw):
    n_tok = u.shape[0]
    g, n_b, p2 = h0.shape
    n_grp = LANES // cw
    wblk = lambda a: pl.BlockSpec((n_grp,) + a.shape[1:], lambda i: (i,) + (0,) * (a.ndim - 1))
    return pl.pallas_call(
        functools.partial(_s5_sample_body, n_grp=n_grp, cw=cw, n_t=n_t, n_b=n_b, p=p2 // 2, row=row),
        grid=(g // n_grp,),
        in_specs=[pl.BlockSpec((n_tok, LANES), lambda i: (0, i)),
                  wblk(h0), wblk(m), wblk(si), wblk(so), wblk(ar), wblk(ai),
                  pl.BlockSpec((1, LANES), lambda i: (0, i))],
        out_specs=[pl.BlockSpec((n_tok, LANES), lambda i: (0, i)),
                   pl.BlockSpec((n_grp, n_b, p2), lambda i: (i, 0, 0))],
        out_shape=[jax.ShapeDtypeStruct((n_tok, g * cw), F32), jax.ShapeDtypeStruct((g, n_b, p2), F32)],
        compiler_params=_params("parallel"),
        name="s5_sample",
    )(u, h0, m, si, so, ar, ai, d_skip.reshape(1, g * cw))


def _mixer_body(z_ref, ub_ref, halo_ref, ga_ref, gb_ref, wglu_ref, bglu_ref, wpool_ref, ps_ref, pa_ref, pb_ref,
                o_ref, ext_scr, *, tm, stride, halo_rows, tiles_per_seq, count_positions):
    i = pl.program_id(0)
    z = z_ref[...]
    glu = z * jax.nn.sigmoid(jnp.dot(z.astype(BF16), wglu_ref[...], preferred_element_type=F32) + bglu_ref[...])
    a = jnp.dot(glu.astype(BF16), pa_ref[...], preferred_element_type=F32)

    ext_scr[0:halo_rows, :] = halo_ref[...]
    if count_positions:
        @pl.when(i % tiles_per_seq == 0)
        def _():
            ext_scr[0:halo_rows, :] = jnp.zeros((halo_rows, ext_scr.shape[1]), F32)
    ext_scr[halo_rows:halo_rows + tm, :] = ub_ref[...]

    gw = wpool_ref.shape[1]
    b = jnp.zeros(o_ref.shape, F32)
    for gi, win in enumerate(POOL_WINDOWS):
        lanes = slice(gi * gw, (gi + 1) * gw)
        u = ext_scr[halo_rows:halo_rows + tm, lanes]
        acc = u
        for k in range(1, win):
            lo = halo_rows - k * stride
            acc = acc + ext_scr[lo:lo + tm, lanes]
        if count_positions:
            pos = (i % tiles_per_seq) * tm + lax.broadcasted_iota(jnp.int32, (tm, gw), 0)
            cnt = jnp.minimum(pos + 1, win).astype(F32)
        else:
            cnt = float(win)
        pooled = acc / cnt - u
        yb = jnp.dot(pooled.astype(BF16), wpool_ref[gi], preferred_element_type=F32) * ps_ref[:, lanes]
        b = b + jnp.dot(yb.astype(BF16), pb_ref[lanes, :], preferred_element_type=F32)
    o_ref[...] = (ga_ref[...].astype(F32) * a + gb_ref[...].astype(F32) * b).astype(BF16)


def _mixer(z, u, halo, gates, w_glu, b_glu, w_pool, pool_scale, p_a, p_b, tm, stride, tiles_per_seq,
           count_positions):
    n_tok, wa = z.shape
    d = p_a.shape[1]
    halo_rows = 16 * stride
    if halo is None:
        per = tm // halo_rows
        halo_arr = u
        halo_spec = pl.BlockSpec((halo_rows, wa), lambda i: (jnp.maximum(i * per - 1, 0), 1))
    else:
        halo_arr = halo
        halo_spec = pl.BlockSpec(halo.shape, lambda i: (0, 0))
    full = lambda a: pl.BlockSpec(a.shape, lambda i: (0,) * a.ndim)
    return pl.pallas_call(
        functools.partial(_mixer_body, tm=tm, stride=stride, halo_rows=halo_rows, tiles_per_seq=tiles_per_seq,
                          count_positions=count_positions),
        grid=(n_tok // tm,),
        in_specs=[pl.BlockSpec((tm, wa), lambda i: (i, 0)),
                  pl.BlockSpec((tm, wa), lambda i: (i, 1)),
                  halo_spec,
                  pl.BlockSpec((tm, d), lambda i: (i, 0)),
                  pl.BlockSpec((tm, d), lambda i: (i, 1)),
                  full(w_glu), pl.BlockSpec((1, wa), lambda i: (0, 0)), full(w_pool),
                  pl.BlockSpec((1, wa), lambda i: (0, 0)), full(p_a), full(p_b)],
        out_specs=pl.BlockSpec((tm, d), lambda i: (i, 0)),
        out_shape=jax.ShapeDtypeStruct((n_tok, d), BF16),
        scratch_shapes=[pltpu.VMEM((halo_rows + tm, wa), F32)],
        compiler_params=_params("arbitrary"),
        name="mixer",
    )(z, u, halo_arr, gates, gates, w_glu, b_glu.reshape(1, wa), w_pool, pool_scale.reshape(1, wa), p_a, p_b)


def _out_proj_body(m_ref, x_ref, gt_ref, sc_ref, sh_ref, g_ref, wo_ref, x1_ref, h2t_ref):
    y = jnp.dot(m_ref[...], wo_ref[...], preferred_element_type=F32)
    x1 = x_ref[...] + gt_ref[0] * y
    x1_ref[...] = x1
    h2 = _rms(x1, g_ref[...]) * (1.0 + sc_ref[0]) + sh_ref[0]
    h2t_ref[...] = h2.T.astype(BF16)


def _out_proj(merged, x, gt, sc, sh, g2, w_o, tm, tiles_per_group):
    n_tok, d = x.shape
    return pl.pallas_call(
        _out_proj_body,
        grid=(n_tok // tm,),
        in_specs=[pl.BlockSpec((tm, d), lambda i: (i, 0)), pl.BlockSpec((tm, d), lambda i: (i, 0)),
                  _mod_spec(gt, tiles_per_group), _mod_spec(sc, tiles_per_group), _mod_spec(sh, tiles_per_group),
                  pl.BlockSpec((1, d), lambda i: (0, 0)), pl.BlockSpec((d, d), lambda i: (0, 0))],
        out_specs=[pl.BlockSpec((tm, d), lambda i: (i, 0)), pl.BlockSpec((d, tm), lambda i: (0, i))],
        out_shape=[jax.ShapeDtypeStruct((n_tok, d), F32), jax.ShapeDtypeStruct((d, n_tok), BF16)],
        compiler_params=_params("parallel"),
        name="out_proj",
    )(merged, x, gt, sc, sh, g2.reshape(1, d), w_o)


def _peer_candidates(topk):
    return [(a, b) for a in range(topk) for b in range(topk) if (a + 1) * (b + 1) <= topk]


def _router_body(h2t_ref, wqt_ref, keys_ref, r1_ref, e1_ref, cnt_ref, e0_ref,
                 q_scr, s_scr, w_scr, rk_scr, t_scr, cn_scr, zi_scr, *, tm, n_head, n_key, half, topk):
    n_lc = tm // LANES
    q_scr[...] = jnp.dot(wqt_ref[...], h2t_ref[...], preferred_element_type=F32).astype(BF16)
    for hk in range(2 * n_head):
        h, k = hk % n_head, hk // n_head
        off = (h * 2 + k) * half
        s_scr[hk] = jnp.dot(keys_ref[hk], q_scr[off:off + half, :], preferred_element_type=F32)

    key_idx = lax.broadcasted_iota(jnp.int32, (n_key, LANES), 0).astype(F32)

    inf = float("inf")
    chunks = [slice(lc * LANES, (lc + 1) * LANES) for lc in range(n_lc)]
    no_rank = jnp.full((n_key, LANES), float(topk), F32)

    def exact_ranks(hk):
        for lanes in chunks:
            def one_round(a, sr, lanes=lanes):
                s, rk = sr
                m = jnp.max(s, axis=0, keepdims=True)
                first = jnp.min(jnp.where(s == m, key_idx, float(n_key)), axis=0, keepdims=True)
                sel = key_idx == first
                t_scr[a, hk:hk + 1, lanes] = m
                return jnp.where(sel, -inf, s), jnp.where(sel, a.astype(F32), rk)

            _, rk = lax.fori_loop(0, topk, one_round, (s_scr[hk, :, lanes], no_rank))
            rk_scr[hk, :, lanes] = rk

    for hk in range(2 * n_head):
        w_scr[...] = s_scr[hk]

        def fast_round(a, rks, hk=hk):
            out = []
            for lanes, rk in zip(chunks, rks):
                s = w_scr[:, lanes]
                m = jnp.max(s, axis=0, keepdims=True)
                sel = s == m
                t_scr[a, hk:hk + 1, lanes] = m
                w_scr[:, lanes] = jnp.where(sel, -inf, s)
                out.append(jnp.where(sel, a.astype(F32), rk))
            return tuple(out)

        rks = lax.fori_loop(0, topk, fast_round, (no_rank,) * n_lc)
        tied = jnp.zeros((1, LANES), F32)
        for lanes, rk in zip(chunks, rks):
            rk_scr[hk, :, lanes] = rk
            n_ranked = jnp.sum(jnp.where(rk < float(topk), 1.0, 0.0), axis=0, keepdims=True)
            tied = jnp.maximum(tied, n_ranked - float(topk))

        @pl.when(jnp.max(tied) > 0.0)
        def _(hk=hk):
            exact_ranks(hk)

    cands = _peer_candidates(topk)
    for lc in range(n_lc):
        lanes = slice(lc * LANES, (lc + 1) * LANES)
        t0 = [t_scr[a, 0:n_head, lanes] for a in range(topk)]
        t1 = [t_scr[b, n_head:2 * n_head, lanes] for b in range(topk)]
        val = [t0[a] + t1[b] for a, b in cands]
        rank = []
        for a, b in cands:
            rank.append(jnp.full((n_head, LANES), float((a + 1) * (b + 1) - 1), F32))
        for c1, (a1, b1) in enumerate(cands):
            for c2 in range(c1 + 1, len(cands)):
                a2, b2 = cands[c2]
                if (a1 <= a2 and b1 <= b2) or (a2 <= a1 and b2 <= b1):
                    continue
                ge = jnp.where(val[c1] >= val[c2], 1.0, 0.0)
                rank[c2] = rank[c2] + ge
                rank[c1] = rank[c1] + (1.0 - ge)
        sel = [jnp.where(r < float(topk), 1.0, 0.0) for r in rank]
        zsum = jnp.zeros((n_head, LANES), F32)
        cnt = [jnp.zeros((n_head, LANES), F32) for _ in range(topk)]
        for c, (a, b) in enumerate(cands):
            zsum = zsum + sel[c] * jnp.exp(val[c] - val[0])
            cnt[a] = cnt[a] + sel[c]
        zi_scr[:, lanes] = 1.0 / zsum
        for a in range(topk):
            cn_scr[a, :, lanes] = cnt[a]

    for h in range(n_head):
        r0 = rk_scr[h]
        c = jnp.zeros((n_key, tm), F32)
        for a in range(topk):
            c = jnp.where(r0 == float(a), cn_scr[a, h:h + 1, :], c)
        cnt_ref[h] = c
        e0_ref[h] = jnp.exp(s_scr[h] - t_scr[0, h:h + 1, :]) * zi_scr[h:h + 1, :]
        e1_ref[h] = jnp.exp(s_scr[n_head + h] - t_scr[0, n_head + h:n_head + h + 1, :]).astype(BF16)
        r1_ref[h] = rk_scr[n_head + h].astype(BF16)


def _router(h2t, wq_t, keys_hk, n_head, tm):
    d, n_tok = h2t.shape
    _, n_key, half = keys_hk.shape
    topk = PEER_TOPK
    out = lambda dt: jax.ShapeDtypeStruct((n_head, n_key, n_tok), dt)
    ospec = pl.BlockSpec((n_head, n_key, tm), lambda i: (0, 0, i))
    return pl.pallas_call(
        functools.partial(_router_body, tm=tm, n_head=n_head, n_key=n_key, half=half, topk=topk),
        grid=(n_tok // tm,),
        in_specs=[pl.BlockSpec((d, tm), lambda i: (0, i)),
                  pl.BlockSpec(wq_t.shape, lambda i: (0, 0)),
                  pl.BlockSpec(keys_hk.shape, lambda i: (0, 0, 0))],
        out_specs=[ospec, ospec, ospec, ospec],
        out_shape=[out(BF16), out(BF16), out(F32), out(F32)],
        scratch_shapes=[pltpu.VMEM((wq_t.shape[0], tm), BF16),
                        pltpu.VMEM((2 * n_head, n_key, tm), F32),
                        pltpu.VMEM((n_key, tm), F32),
                        pltpu.VMEM((2 * n_head, n_key, tm), F32),
                        pltpu.VMEM((topk, 2 * n_head, tm), F32),
                        pltpu.VMEM((topk, n_head, tm), F32),
                        pltpu.VMEM((n_head, tm), F32)],
        compiler_params=_params("parallel"),
        name="router",
    )(h2t, wq_t, keys_hk)


def _peer_body(h2t_ref, u_ref, v_ref, r1_ref, e1_ref, cnt_ref, e0_ref, o_ref, *, ni, n_sub, n_key, n_head, tm):
    e = pl.program_id(1)

    @pl.when(e == 0)
    def _():
        o_ref[...] = jnp.zeros(o_ref.shape, F32)

    base = pl.multiple_of(e * ni, ni)
    cblk = [cnt_ref[h, pl.ds(base, ni), :] for h in range(n_head)]
    eblk = [e0_ref[h, pl.ds(base, ni), :] for h in range(n_head)]
    acc = None
    for j0 in range(0, ni, n_sub):
        rows = slice(j0 * n_key, (j0 + n_sub) * n_key)
        act = jnp.dot(u_ref[rows, :], h2t_ref[...], preferred_element_type=F32)
        gated = []
        for s in range(n_sub):
            ii = j0 + s
            w = jnp.zeros((n_key, tm), BF16)
            for h in range(n_head):
                cb = jnp.broadcast_to(cblk[h][ii:ii + 1, :], (n_key, tm)).astype(BF16)
                eb = jnp.broadcast_to(eblk[h][ii:ii + 1, :], (n_key, tm)).astype(BF16)
                w = w + jnp.where(r1_ref[h] < cb, e1_ref[h], jnp.zeros((), BF16)) * eb
            gated.append(_gelu(act[s * n_key:(s + 1) * n_key, :]).astype(BF16) * w)
        part = lax.dot_general(jnp.concatenate(gated, axis=0), v_ref[rows, :], (((0,), (0,)), ((), ())),
                               preferred_element_type=F32)
        acc = part if acc is None else acc + part
    o_ref[...] += acc


def _peer(h2t, u_bf, v_bf, r1, e1, cnt, e0, tm, ni):
    d, n_tok = h2t.shape
    n_head, n_key, _ = r1.shape
    te = ni * n_key
    hspec = pl.BlockSpec((n_head, n_key, tm), lambda t, e: (0, 0, t))
    return pl.pallas_call(
        functools.partial(_peer_body, ni=ni, n_sub=ni // 2, n_key=n_key, n_head=n_head, tm=tm),
        grid=(n_tok // tm, n_key // ni),
        in_specs=[pl.BlockSpec((d, tm), lambda t, e: (0, t)),
                  pl.BlockSpec((te, d), lambda t, e: (e, 0)),
                  pl.BlockSpec((te, d), lambda t, e: (e, 0)),
                  hspec, hspec, hspec, hspec],
        out_specs=pl.BlockSpec((tm, d), lambda t, e: (t, 0)),
        out_shape=jax.ShapeDtypeStruct((n_tok, d), F32),
        compiler_params=_params("parallel", "arbitrary"),
        name="peer",
    )(h2t, u_bf, v_bf, r1, e1, cnt, e0)


def _final_body(x1_ref, p_ref, gt_ref, gf_ref, o_ref, *, last):
    x2 = x1_ref[...] + gt_ref[0] * p_ref[...]
    o_ref[...] = _rms(x2, gf_ref[...]) if last else x2


def _final(x1, peer_out, row_off, gt, g_final, tm, tiles_per_group, last):
    n_tok, d = x1.shape
    off = row_off // tm
    return pl.pallas_call(
        functools.partial(_final_body, last=last),
        grid=(n_tok // tm,),
        in_specs=[pl.BlockSpec((tm, d), lambda i: (i, 0)), pl.BlockSpec((tm, d), lambda i: (i + off, 0)),
                  _mod_spec(gt, tiles_per_group), pl.BlockSpec((1, d), lambda i: (0, 0))],
        out_specs=pl.BlockSpec((tm, d), lambda i: (i, 0)),
        out_shape=jax.ShapeDtypeStruct((n_tok, d), F32),
        compiler_params=_params("parallel"),
        name="final",
    )(x1, peer_out, gt, g_final.reshape(1, d))


def kernel(x_prompt, x_sample, state_s5_re, state_s5_im, state_pool, c_prompt, c_sample, w_ada, b_ada, g_norm1, g_norm2, w_in, s5_a_re, s5_a_im, s5_log_dt, s5_b_re, s5_b_im, s5_c_re, s5_c_im, s5_d, w_glu, b_glu, w_pool, pool_scale, w_proj_a, w_proj_b, w_out, peer_wq, peer_keys, peer_u, peer_v, g_final):
    depth = w_ada.shape[0]
    nb, nt, d = x_prompt.shape
    sb, st, _ = x_sample.shape
    n_grp, n_state, grp_w = s5_b_re.shape[1:]
    s5_w = n_grp * grp_w
    n_head = peer_keys.shape[1]
    n_key = peer_keys.shape[3]
    pool_buf = state_pool.shape[2]
    n_p, n_s = nb * nt, sb * st
    n_chunk = nt // S5_CHUNK
    n_lvl = int(math.log2(n_chunk))
    assert 2 ** n_lvl == n_chunk and st <= S5_CHUNK and pool_buf == max(POOL_WINDOWS) - 1

    tm_a, tm_b, tm_r, tm_e = 512, 256, 256, 512
    assert nt % tm_a == 0 and n_s % tm_a == 0 and (n_p + n_s) % tm_e == 0

    xp = x_prompt.reshape(n_p, d)
    xs = jnp.swapaxes(x_sample, 0, 1).reshape(n_s, d)
    c_all = jnp.concatenate([c_prompt, c_sample], axis=0)
    c_all = jnp.pad(c_all, ((0, (-c_all.shape[0]) % 8), (0, 0)))

    new_p = {"re": [], "im": [], "buf": []}
    new_s = {"re": [], "im": [], "buf": []}
    for l in range(depth):
        last = l == depth - 1
        mod = _ada(c_all, w_ada[l], b_ada[l])
        mods_p = [m.reshape(nb, 1, d) for m in jnp.split(mod[:nb], 6, axis=-1)]
        mods_s_rows = [jnp.tile(m, (st, 1)) for m in jnp.split(mod[nb:nb + sb], 6, axis=-1)]
        mods_s = lambda tm: [m.reshape(n_s // tm, tm, d) for m in mods_s_rows]

        w_in_bf = w_in[l].astype(BF16)
        u_p, gates_p = _in_proj(xp, mods_p[1], mods_p[0], g_norm1[l], w_in_bf, 2 * s5_w, tm_a, nt // tm_a)
        ms = mods_s(tm_a)
        u_s, gates_s = _in_proj(xs, ms[1], ms[0], g_norm1[l], w_in_bf, 2 * s5_w, tm_a, 1)

        k_out, e_re, e_im, f_re, f_im, pw_re, pw_im = _s5_prep(
            s5_a_re[l], s5_a_im[l], s5_log_dt[l], s5_b_re[l], s5_b_im[l], s5_c_re[l], s5_c_im[l],
            S5_CHUNK, n_lvl, st)
        ar = jnp.concatenate([pw_re, pw_re], axis=-1)
        ai = jnp.concatenate([-pw_im, pw_im], axis=-1)
        m16, si16, so16 = _s5_matrices(k_out, e_re, e_im, f_re, f_im, S5_CHUNK)
        m4, si4, so4 = _s5_matrices(k_out, e_re, e_im, f_re, f_im, st)

        z_p, hf_p = _s5_prompt(u_p, m16, si16, so16, ar, ai, s5_d[l], nb, S5_CHUNK, grp_w)
        hf_p = hf_p.transpose(1, 0, 2)
        new_p["re"].append(hf_p[..., :n_state])
        new_p["im"].append(hf_p[..., n_state:])

        h0 = jnp.concatenate([state_s5_re[l], state_s5_im[l]], axis=-1).transpose(1, 0, 2)
        z_s, hn_s = _s5_sample(u_s, h0, m4, si4, so4, ar, ai, s5_d[l], n_lvl, st, grp_w)
        hn_s = hn_s.transpose(1, 0, 2)
        new_s["re"].append(hn_s[..., :n_state])
        new_s["im"].append(hn_s[..., n_state:])

        new_p["buf"].append(u_p.reshape(nb, nt, 2 * s5_w)[:, nt - pool_buf:, s5_w:])
        hist = jnp.swapaxes(state_pool[l], 0, 1)
        ub_s = u_s[:, s5_w:].reshape(st, sb, s5_w)
        new_s["buf"].append(jnp.swapaxes(jnp.concatenate([hist, ub_s], axis=0)[st:], 0, 1))
        halo_s = jnp.concatenate([jnp.zeros((1, sb, s5_w), F32), hist], axis=0).reshape((pool_buf + 1) * sb, s5_w)

        wb = lambda a: a.astype(BF16)
        mix_w = (wb(w_glu[l]), b_glu[l], wb(w_pool[l]), pool_scale[l], wb(w_proj_a[l]), wb(w_proj_b[l]))
        merged_p = _mixer(z_p, u_p, None, gates_p, *mix_w, tm=tm_b, stride=1, tiles_per_seq=nt // tm_b,
                          count_positions=True)
        merged_s = _mixer(z_s, u_s, halo_s, gates_s, *mix_w, tm=n_s, stride=sb, tiles_per_seq=1,
                          count_positions=False)

        w_o_bf = wb(w_out[l])
        x1_p, h2t_p = _out_proj(merged_p, xp, mods_p[2], mods_p[4], mods_p[3], g_norm2[l], w_o_bf, tm_b, nt // tm_b)
        ms = mods_s(tm_b)
        x1_s, h2t_s = _out_proj(merged_s, xs, ms[2], ms[4], ms[3], g_norm2[l], w_o_bf, tm_b, 1)

        h2t = jnp.concatenate([h2t_p, h2t_s], axis=1)
        keys_hk = wb(jnp.swapaxes(peer_keys[l], 0, 1).reshape(2 * n_head, n_key, -1))
        r1, e1, cnt, e0 = _router(h2t, wb(peer_wq[l].T), keys_hk, n_head, tm_r)
        peer_out = _peer(h2t, wb(peer_u[l]), wb(peer_v[l]), r1, e1, cnt, e0, tm_e, 8)

        xp = _final(x1_p, peer_out, 0, mods_p[5], g_final, tm_a, nt // tm_a, last)
        xs = _final(x1_s, peer_out, n_p, mods_s(tm_a)[5], g_final, tm_a, 1, last)

    y_prompt = xp.reshape(nb, nt, d)
    y_sample = jnp.swapaxes(xs.reshape(st, sb, d), 0, 1)
    stack = lambda xs_: jnp.stack(xs_)
    return (y_prompt, y_sample, stack(new_p["re"]), stack(new_p["im"]), stack(new_p["buf"]),
            stack(new_s["re"]), stack(new_s["im"]), stack(new_s["buf"]))
```

```python
import functools
import math

import numpy as np
import jax
import jax.numpy as jnp
from jax import lax
from jax.experimental import pallas as pl
from jax.experimental.pallas import tpu as pltpu

F32 = jnp.float32
BF16 = jnp.bfloat16
EPS = 1e-6
PAST_LEN = 16384
POOL_WINDOWS = (2, 4, 8, 16)
PEER_TOPK = 16
S5_CHUNK = 16
LANES = 128
SUBLANES = 8
BF16_ROWS = 16
ROW_BLOCK = 256
VMEM_LIMIT = 56 * 1024 * 1024


def _gelu(x):
    return 0.5 * x * (1.0 + lax.erf(x * (1.0 / math.sqrt(2.0))))


def _rms(x, g):
    ms = jnp.mean(x * x, axis=-1, keepdims=True)
    return x * lax.rsqrt(ms + EPS) * g


def _params(*sem):
    return pltpu.CompilerParams(dimension_semantics=sem, vmem_limit_bytes=VMEM_LIMIT)


def _mod_rows(ref, tm):
    m = ref[0]
    reps = tm // m.shape[0] if m.shape[0] > 1 else 1
    return m if reps == 1 else jnp.concatenate([m] * reps, axis=0)


def _mod_spec(arr, tiles_per_group):
    _, r, d = arr.shape
    return pl.BlockSpec((1, r, d), lambda i, *_: (i // tiles_per_group, 0, 0))


def _ada_body(c_ref, w_ref, b_ref, o_ref):
    c = c_ref[...]
    s = (c * jax.nn.sigmoid(c)).astype(BF16)
    o_ref[...] = jnp.dot(s, w_ref[...].astype(BF16), preferred_element_type=F32) + b_ref[...]


def _ada(c, w, b, tn=1024):
    m, d = c.shape
    n = w.shape[1]
    return pl.pallas_call(
        _ada_body,
        grid=(n // tn,),
        in_specs=[pl.BlockSpec((m, d), lambda j: (0, 0)),
                  pl.BlockSpec((d, tn), lambda j: (0, j)),
                  pl.BlockSpec((1, tn), lambda j: (0, j))],
        out_specs=pl.BlockSpec((m, tn), lambda j: (0, j)),
        out_shape=jax.ShapeDtypeStruct((m, n), F32),
        compiler_params=_params("parallel"),
        name="ada",
    )(c, w, b.reshape(1, n))


def _in_proj_body(x_ref, sc_ref, sh_ref, g_ref, w_ref, u_ref, gate_ref, h_scr, *, n_u, n_sub):
    n = pl.program_id(1)

    @pl.when(n == 0)
    def _():
        tm = x_ref.shape[0]
        h = _rms(x_ref[...], g_ref[...]) * (1.0 + _mod_rows(sc_ref, tm)) + _mod_rows(sh_ref, tm)
        h_scr[...] = h.astype(BF16)

    tn = w_ref.shape[1]
    cols = [slice(c, c + n_sub) for c in range(0, tn, n_sub)]

    @pl.when(n < n_u)
    def _():
        for c in cols:
            u_ref[:, c] = jnp.dot(h_scr[...], w_ref[:, c], preferred_element_type=F32)

    @pl.when(n >= n_u)
    def _():
        for c in cols:
            acc = jnp.dot(h_scr[...], w_ref[:, c], preferred_element_type=F32)
            gate_ref[:, c] = jax.nn.sigmoid(acc).astype(BF16)


def _in_proj(x, sc, sh, g1, w_bf, n_lin, tm, tiles_per_group):
    n_tok, d = x.shape
    n_all = w_bf.shape[1]
    tn = n_lin // 2
    n_u = n_lin // tn
    return pl.pallas_call(
        functools.partial(_in_proj_body, n_u=n_u, n_sub=256),
        grid=(n_tok // tm, n_all // tn),
        in_specs=[pl.BlockSpec((tm, d), lambda i, n: (i, 0)),
                  _mod_spec(sc, tiles_per_group), _mod_spec(sh, tiles_per_group),
                  pl.BlockSpec((1, d), lambda i, n: (0, 0)),
                  pl.BlockSpec((d, tn), lambda i, n: (0, n))],
        out_specs=[pl.BlockSpec((tm, tn), lambda i, n: (i, jnp.minimum(n, n_u - 1))),
                   pl.BlockSpec((tm, tn), lambda i, n: (i, jnp.maximum(n - n_u, 0)))],
        out_shape=[jax.ShapeDtypeStruct((n_tok, n_lin), F32),
                   jax.ShapeDtypeStruct((n_tok, n_all - n_lin), BF16)],
        scratch_shapes=[pltpu.VMEM((tm, d), BF16)],
        compiler_params=_params("parallel", "arbitrary"),
        name="in_proj",
    )(x, sc, sh, g1.reshape(1, d), w_bf)


def _s5_prep_body(are_ref, aim_ref, ldt_ref, btre_ref, btim_ref, cre_ref, cim_ref,
                  k_ref, ere_ref, eim_ref, fre_ref, fim_ref, pwre_ref, pwim_ref, *, gb, n_lag, n_lvl, small_pow):
    hi = lax.Precision.HIGHEST
    nt = (((1,), (1,)), ((), ()))
    for j in range(gb):
        a_re, a_im = are_ref[j], aim_ref[j]
        dt = jnp.exp(ldt_ref[j])
        lr, li = a_re * dt, a_im * dt
        mag = jnp.exp(lr)
        ab_re, ab_im = mag * jnp.cos(li), mag * jnp.sin(li)
        num_re, num_im = ab_re - 1.0, ab_im
        den = a_re * a_re + a_im * a_im
        f_re = (num_re * a_re + num_im * a_im) / den
        f_im = (num_im * a_re - num_re * a_im) / den
        bt_re, bt_im = btre_ref[j], btim_ref[j]
        bb_re = f_re * bt_re - f_im * bt_im
        bb_im = f_re * bt_im + f_im * bt_re
        c_re, c_im = cre_ref[j], cim_ref[j]
        pows = [(jnp.ones_like(ab_re), jnp.zeros_like(ab_re))]
        for _ in range(n_lag):
            p_re, p_im = pows[-1]
            pows.append((p_re * ab_re - p_im * ab_im, p_re * ab_im + p_im * ab_re))
        g_re, g_im = [], []
        for k in range(n_lag):
            p_re, p_im = pows[k]
            ere_ref[j, k] = p_re * bb_re - p_im * bb_im
            eim_ref[j, k] = p_re * bb_im + p_im * bb_re
            g_re.append(c_re * p_re - c_im * p_im)
            g_im.append(c_re * p_im + c_im * p_re)
            q_re, q_im = pows[k + 1]
            fre_ref[j, k] = c_re * q_re - c_im * q_im
            fim_ref[j, k] = c_re * q_im + c_im * q_re
        gr = jnp.concatenate(g_re, axis=0)
        gi = jnp.concatenate(g_im, axis=0)
        k_ref[j] = (lax.dot_general(gr, bb_re, nt, precision=hi, preferred_element_type=F32)
                    - lax.dot_general(gi, bb_im, nt, precision=hi, preferred_element_type=F32))
        p_re, p_im = pows[n_lag]
        for lvl in range(n_lvl):
            pwre_ref[j, lvl:lvl + 1, :] = p_re
            pwim_ref[j, lvl:lvl + 1, :] = p_im
            p_re, p_im = p_re * p_re - p_im * p_im, 2.0 * p_re * p_im
        q_re, q_im = pows[small_pow]
        pwre_ref[j, n_lvl:n_lvl + 1, :] = q_re
        pwim_ref[j, n_lvl:n_lvl + 1, :] = q_im


def _s5_prep(a_re, a_im, log_dt, b_re, b_im, c_re, c_im, n_lag, n_lvl, small_pow, gb=8):
    g, p = a_re.shape
    c = b_re.shape[-1]
    row = lambda a: a.reshape(g, 1, -1)
    bt = lambda a: jnp.swapaxes(a, 1, 2)
    vec = lambda w: pl.BlockSpec((gb, 1, w), lambda i: (i, 0, 0))
    mat = pl.BlockSpec((gb, c, p), lambda i: (i, 0, 0))
    lag = pl.BlockSpec((gb, n_lag, c, p), lambda i: (i, 0, 0, 0))
    lag_shape = jax.ShapeDtypeStruct((g, n_lag, c, p), F32)
    pw_shape = jax.ShapeDtypeStruct((g, n_lvl + 1, p), F32)
    pw = pl.BlockSpec((gb, n_lvl + 1, p), lambda i: (i, 0, 0))
    return pl.pallas_call(
        functools.partial(_s5_prep_body, gb=gb, n_lag=n_lag, n_lvl=n_lvl, small_pow=small_pow),
        grid=(g // gb,),
        in_specs=[vec(p), vec(p), vec(1), mat, mat, mat, mat],
        out_specs=[pl.BlockSpec((gb, n_lag * c, c), lambda i: (i, 0, 0)), lag, lag, lag, lag, pw, pw],
        out_shape=[jax.ShapeDtypeStruct((g, n_lag * c, c), F32), lag_shape, lag_shape, lag_shape, lag_shape,
                   pw_shape, pw_shape],
        compiler_params=_params("parallel"),
        name="s5_prep",
    )(row(a_re), row(a_im), row(log_dt), bt(b_re), bt(b_im), c_re, c_im)


def _s5_matrices(k_out, e_re, e_im, f_re, f_im, length):
    g, n_lag, c, p = e_re.shape
    pad = -(length * c) % LANES
    kt = k_out.reshape(g, n_lag, c, c).transpose(0, 3, 1, 2).reshape(g, c, n_lag * c)
    rows = [jnp.pad(kt[:, :, :(length - s) * c], ((0, 0), (0, 0), (s * c, 0))) for s in range(length)]
    toep = jnp.stack(rows, axis=1).reshape(g, length * c, length * c)
    rev = lambda a: jnp.flip(a[:, :length], axis=1)
    si = jnp.concatenate([rev(e_re), rev(e_im)], axis=-1).reshape(g, length * c, 2 * p)
    so_re = f_re[:, :length].transpose(0, 3, 1, 2).reshape(g, p, length * c)
    so_im = f_im[:, :length].transpose(0, 3, 1, 2).reshape(g, p, length * c)
    so = jnp.concatenate([so_re, -so_im], axis=1)
    toep = jnp.pad(toep, ((0, 0), (0, pad), (0, pad)))
    si = jnp.pad(si, ((0, 0), (0, pad), (0, 0)))
    so = jnp.pad(so, ((0, 0), (0, 0), (0, pad)))
    return toep.astype(BF16), si.astype(BF16), so.astype(BF16)


def _to_chunk_rows(u_t, n_grp, cw):
    per = LANES // cw
    blk = lax.broadcasted_iota(jnp.int32, u_t[0].shape, 1) // cw
    out = []
    for g in range(n_grp):
        cols = []
        for t0 in range(0, len(u_t), per):
            acc = None
            for tt, u in enumerate(u_t[t0:t0 + per]):
                shift = ((tt - g) * cw) % LANES
                piece = pltpu.roll(u, shift, 1) if shift else u
                acc = piece if acc is None else jnp.where(blk == tt, piece, acc)
            cols.append(acc)
        out.append(cols[0] if len(cols) == 1 else jnp.concatenate(cols, axis=1))
    return out


def _to_token_rows(y_g, n_t, cw):
    per = LANES // cw
    blk = lax.broadcasted_iota(jnp.int32, (y_g[0].shape[0], LANES), 1) // cw
    out = []
    for t in range(n_t):
        col, tt = t // per, t % per
        acc = None
        for g, y in enumerate(y_g):
            src = y[:, col * LANES:(col + 1) * LANES]
            shift = ((g - tt) * cw) % LANES
            piece = pltpu.roll(src, shift, 1) if shift else src
            acc = piece if acc is None else jnp.where(blk == g, piece, acc)
        out.append(acc)
    return out


def _s5_prompt_body(u_ref, mt_ref, sit_ref, sot_ref, ar_ref, ai_ref, d_ref, z_ref, hf_ref,
                    *, n_grp, cw, n_t, n_seq, n_chunk, p):
    rows = n_seq * n_chunk
    u_t = [u_ref[pl.ds(t, rows, stride=n_t), :] for t in range(n_t)]
    ut_t = [u.T for u in u_t]
    ridx = lax.broadcasted_iota(jnp.int32, (rows, 2 * p), 0) % n_chunk
    yt_g = []
    for j in range(n_grp):
        xt = jnp.concatenate([ut[j * cw:(j + 1) * cw, :] for ut in ut_t], axis=0).astype(BF16)
        h = jnp.dot(sit_ref[j], xt, preferred_element_type=F32).T
        d, lvl = 1, 0
        while d < n_chunk:
            sh = jnp.where(ridx >= d, pltpu.roll(h, d, 0), 0.0)
            sw = pltpu.roll(sh, p, 1)
            h = h + ar_ref[j, lvl:lvl + 1, :] * sh + ai_ref[j, lvl:lvl + 1, :] * sw
            d, lvl = 2 * d, lvl + 1
        hs = jnp.where(ridx >= 1, pltpu.roll(h, 1, 0), 0.0)
        yt_g.append(jnp.dot(mt_ref[j], xt, preferred_element_type=F32)
                    + jnp.dot(sot_ref[j], hs.T.astype(BF16), preferred_element_type=F32))
        for b in range(n_seq):
            hf_ref[j, b:b + 1, :] = h[(b + 1) * n_chunk - 1:(b + 1) * n_chunk, :]
    for t in range(n_t):
        yt = jnp.concatenate([y[t * cw:(t + 1) * cw, :] for y in yt_g], axis=0)
        z_ref[pl.ds(t, rows, stride=n_t), :] = _gelu(yt.T + d_ref[...] * u_t[t])


def _s5_prompt(u, m, si, so, ar, ai, d_skip, n_seq, n_t, cw):
    n_tok = u.shape[0]
    g, p2, _ = si.shape
    n_grp = LANES // cw
    n_chunk = n_tok // (n_seq * n_t)
    wblk = lambda a: pl.BlockSpec((n_grp,) + a.shape[1:], lambda i: (i,) + (0,) * (a.ndim - 1))
    return pl.pallas_call(
        functools.partial(_s5_prompt_body, n_grp=n_grp, cw=cw, n_t=n_t, n_seq=n_seq, n_chunk=n_chunk, p=p2 // 2),
        grid=(g // n_grp,),
        in_specs=[pl.BlockSpec((n_tok, LANES), lambda i: (0, i)),
                  wblk(m), wblk(si), wblk(so), wblk(ar), wblk(ai),
                  pl.BlockSpec((1, LANES), lambda i: (0, i))],
        out_specs=[pl.BlockSpec((n_tok, LANES), lambda i: (0, i)),
                   pl.BlockSpec((n_grp, n_seq, p2), lambda i: (i, 0, 0))],
        out_shape=[jax.ShapeDtypeStruct((n_tok, g * cw), F32), jax.ShapeDtypeStruct((g, n_seq, p2), F32)],
        compiler_params=_params("parallel"),
        name="s5_prompt",
    )(u, m, si, so, ar, ai, d_skip.reshape(1, g * cw))


def _s5_sample_body(u_ref, h0_ref, m_ref, si_ref, so_ref, ar_ref, ai_ref, d_ref, z_ref, hn_ref,
                    *, n_grp, cw, n_t, n_b, p, row):
    u_t = [u_ref[t * n_b:(t + 1) * n_b, :] for t in range(n_t)]
    x_g = _to_chunk_rows(u_t, n_grp, cw)
    y_g = []
    for j in range(n_grp):
        xb = x_g[j].astype(BF16)
        h0 = h0_ref[j]
        y_g.append(jnp.dot(xb, m_ref[j], preferred_element_type=F32)
                   + jnp.dot(h0.astype(BF16), so_ref[j], preferred_element_type=F32))
        hn_ref[j] = (ar_ref[j, row:row + 1, :] * h0 + ai_ref[j, row:row + 1, :] * pltpu.roll(h0, p, 1)
                     + jnp.dot(xb, si_ref[j], preferred_element_type=F32))
    for t, y in enumerate(_to_token_rows(y_g, n_t, cw)):
        z_ref[t * n_b:(t + 1) * n_b, :] = _gelu(y + d_ref[...] * u_t[t])


def _s5_sample(u, h0, m, si, so, ar, ai, d_skip, row, n_t, cw):
    n_tok = u.shape[0]
    g, n_b, p2 = h0.shape
    n_grp = LANES // cw
    wblk = lambda a: pl.BlockSpec((n_grp,) + a.shape[1:], lambda i: (i,) + (0,) * (a.ndim - 1))
    return pl.pallas_call(
        functools.partial(_s5_sample_body, n_grp=n_grp, cw=cw, n_t=n_t, n_b=n_b, p=p2 // 2, row=row),
        grid=(g // n_grp,),
        in_specs=[pl.BlockSpec((n_tok, LANES), lambda i: (0, i)),
                  wblk(h0), wblk(m), wblk(si), wblk(so), wblk(ar), wblk(ai),
                  pl.BlockSpec((1, LANES), lambda i: (0, i))],
        out_specs=[pl.BlockSpec((n_tok, LANES), lambda i: (0, i)),
                   pl.BlockSpec((n_grp, n_b, p2), lambda i: (i, 0, 0))],
        out_shape=[jax.ShapeDtypeStruct((n_tok, g * cw), F32), jax.ShapeDtypeStruct((g, n_b, p2), F32)],
        compiler_params=_params("parallel"),
        name="s5_sample",
    )(u, h0, m, si, so, ar, ai, d_skip.reshape(1, g * cw))


def _mixer_body(z_ref, ub_ref, halo_ref, ga_ref, gb_ref, wglu_ref, bglu_ref, wpool_ref, ps_ref, pa_ref, pb_ref,
                o_ref, ext_scr, *, tm, stride, halo_rows, tiles_per_seq, count_positions):
    i = pl.program_id(0)
    ext_scr[0:halo_rows, :] = halo_ref[...]
    if count_positions:
        @pl.when(i % tiles_per_seq == 0)
        def _():
            ext_scr[0:halo_rows, :] = jnp.zeros((halo_rows, ext_scr.shape[1]), F32)
    ext_scr[halo_rows:halo_rows + tm, :] = ub_ref[...]

    gw = wpool_ref.shape[1]
    rb = min(tm, ROW_BLOCK)
    for r0 in range(0, tm, rb):
        rows = slice(r0, r0 + rb)
        z = z_ref[rows, :]
        glu = z * jax.nn.sigmoid(jnp.dot(z.astype(BF16), wglu_ref[...], preferred_element_type=F32) + bglu_ref[...])
        a = jnp.dot(glu.astype(BF16), pa_ref[...], preferred_element_type=F32)
        b = jnp.zeros((rb, o_ref.shape[1]), F32)
        for gi, win in enumerate(POOL_WINDOWS):
            lanes = slice(gi * gw, (gi + 1) * gw)
            u = ext_scr[halo_rows + r0:halo_rows + r0 + rb, lanes]
            acc = u
            for k in range(1, win):
                lo = halo_rows + r0 - k * stride
                acc = acc + ext_scr[lo:lo + rb, lanes]
            if count_positions:
                pos = (i % tiles_per_seq) * tm + r0 + lax.broadcasted_iota(jnp.int32, (rb, gw), 0)
                cnt = jnp.minimum(pos + 1, win).astype(F32)
            else:
                cnt = float(win)
            pooled = acc / cnt - u
            yb = jnp.dot(pooled.astype(BF16), wpool_ref[gi], preferred_element_type=F32) * ps_ref[:, lanes]
            b = b + jnp.dot(yb.astype(BF16), pb_ref[lanes, :], preferred_element_type=F32)
        o_ref[rows, :] = (ga_ref[rows, :].astype(F32) * a + gb_ref[rows, :].astype(F32) * b).astype(BF16)


def _mixer(z, u, halo, gates, w_glu, b_glu, w_pool, pool_scale, p_a, p_b, tm, stride, tiles_per_seq,
           count_positions):
    n_tok, wa = z.shape
    d = p_a.shape[1]
    halo_rows = -(-(max(POOL_WINDOWS) - 1) * stride // 8) * 8
    if halo is None:
        per = tm // halo_rows
        halo_arr = u
        halo_spec = pl.BlockSpec((halo_rows, wa), lambda i: (jnp.maximum(i * per - 1, 0), 1))
    else:
        halo_arr = halo
        halo_spec = pl.BlockSpec(halo.shape, lambda i: (0, 0))
    full = lambda a: pl.BlockSpec(a.shape, lambda i: (0,) * a.ndim)
    return pl.pallas_call(
        functools.partial(_mixer_body, tm=tm, stride=stride, halo_rows=halo_rows, tiles_per_seq=tiles_per_seq,
                          count_positions=count_positions),
        grid=(n_tok // tm,),
        in_specs=[pl.BlockSpec((tm, wa), lambda i: (i, 0)),
                  pl.BlockSpec((tm, wa), lambda i: (i, 1)),
                  halo_spec,
                  pl.BlockSpec((tm, d), lambda i: (i, 0)),
                  pl.BlockSpec((tm, d), lambda i: (i, 1)),
                  full(w_glu), pl.BlockSpec((1, wa), lambda i: (0, 0)), full(w_pool),
                  pl.BlockSpec((1, wa), lambda i: (0, 0)), full(p_a), full(p_b)],
        out_specs=pl.BlockSpec((tm, d), lambda i: (i, 0)),
        out_shape=jax.ShapeDtypeStruct((n_tok, d), BF16),
        scratch_shapes=[pltpu.VMEM((halo_rows + tm, wa), F32)],
        compiler_params=_params("arbitrary"),
        name="mixer",
    )(z, u, halo_arr, gates, gates, w_glu, b_glu.reshape(1, wa), w_pool, pool_scale.reshape(1, wa), p_a, p_b)


def _out_proj_body(m_ref, x_ref, gt_ref, sc_ref, sh_ref, g_ref, wo_ref, x1_ref, h2t_ref):
    tm = x_ref.shape[0]
    gt, sc, sh = _mod_rows(gt_ref, tm), _mod_rows(sc_ref, tm), _mod_rows(sh_ref, tm)
    rb = min(tm, ROW_BLOCK)
    for r0 in range(0, tm, rb):
        rows = slice(r0, r0 + rb)
        mrow = (lambda m: m if m.shape[0] == 1 else m[rows, :])
        y = jnp.dot(m_ref[rows, :], wo_ref[...], preferred_element_type=F32)
        x1 = x_ref[rows, :] + mrow(gt) * y
        x1_ref[rows, :] = x1
        h2 = _rms(x1, g_ref[...]) * (1.0 + mrow(sc)) + mrow(sh)
        h2t_ref[:, rows] = h2.T.astype(BF16)


def _out_proj(merged, x, gt, sc, sh, g2, w_o, tm, tiles_per_group):
    n_tok, d = x.shape
    return pl.pallas_call(
        _out_proj_body,
        grid=(n_tok // tm,),
        in_specs=[pl.BlockSpec((tm, d), lambda i: (i, 0)), pl.BlockSpec((tm, d), lambda i: (i, 0)),
                  _mod_spec(gt, tiles_per_group), _mod_spec(sc, tiles_per_group), _mod_spec(sh, tiles_per_group),
                  pl.BlockSpec((1, d), lambda i: (0, 0)), pl.BlockSpec((d, d), lambda i: (0, 0))],
        out_specs=[pl.BlockSpec((tm, d), lambda i: (i, 0)), pl.BlockSpec((d, tm), lambda i: (0, i))],
        out_shape=[jax.ShapeDtypeStruct((n_tok, d), F32), jax.ShapeDtypeStruct((d, n_tok), BF16)],
        compiler_params=_params("parallel"),
        name="out_proj",
    )(merged, x, gt, sc, sh, g2.reshape(1, d), w_o)


def _peer_candidates(topk):
    return [(a, b) for a in range(topk) for b in range(topk) if (a + 1) * (b + 1) <= topk]


def _sort_pairs(n):
    pairs = []

    def merge(lo, m, r):
        step = r * 2
        if step < m:
            merge(lo, m, step)
            merge(lo + r, m, step)
            pairs.extend((i, i + r) for i in range(lo + r, lo + m - r, step))
        else:
            pairs.append((lo, lo + r))

    def sort(lo, m):
        if m > 1:
            sort(lo, m // 2)
            sort(lo + m // 2, m // 2)
            merge(lo, m, 1)

    sort(0, n)
    return pairs


def _sorted_top(keys):
    n = len(keys)
    v = list(keys)
    for i, j in _sort_pairs(n):
        v[i], v[j] = jnp.maximum(v[i], v[j]), jnp.minimum(v[i], v[j])
    shift = SUBLANES // 2
    while shift >= 1:
        v = [jnp.maximum(v[k], pltpu.roll(v[n - 1 - k], shift, 0)) for k in range(n)]
        stride = n // 2
        while stride >= 1:
            for i in range(n):
                if i & stride == 0:
                    v[i], v[i + stride] = jnp.maximum(v[i], v[i + stride]), jnp.minimum(v[i], v[i + stride])
            stride //= 2
        shift //= 2
    return v


def _count_greater(top, x):
    n = len(top)
    count = jnp.zeros(x.shape, F32)
    step = n // 2
    bits = []
    while step >= 1:
        cand = [top[m * 2 * step + step - 1] for m in range(n // (2 * step))]
        for b in bits:
            cand = [jnp.where(b, cand[2 * m + 1], cand[2 * m]) for m in range(len(cand) // 2)]
        bit = cand[0] > x
        count = count + jnp.where(bit, float(step), 0.0)
        bits.insert(0, bit)
        step //= 2
    return count + jnp.where(top[n - 1] > x, 1.0, 0.0)


def _router_body(h2t_ref, wqt_ref, keys_ref, r1_ref, e1_ref, cnt_ref, e0_ref,
                 q_scr, s_scr, rk_scr, t_scr, cn_scr, th_scr, zi_scr, *, tm, n_head, n_key, half, topk):
    assert n_key == topk * SUBLANES
    n_lc = tm // LANES
    q_scr[...] = jnp.dot(wqt_ref[...], h2t_ref[...], preferred_element_type=F32).astype(BF16)
    for hk in range(2 * n_head):
        h, k = hk % n_head, hk // n_head
        off = (h * 2 + k) * half
        s_scr[hk] = jnp.dot(keys_ref[hk], q_scr[off:off + half, :], preferred_element_type=F32)

    key_idx = lax.broadcasted_iota(jnp.int32, (n_key, LANES), 0).astype(F32)

    inf = float("inf")
    chunks = [slice(lc * LANES, (lc + 1) * LANES) for lc in range(n_lc)]
    no_rank = jnp.full((n_key, LANES), float(topk), F32)

    def exact_ranks(hk):
        for lanes in chunks:
            def one_round(a, sr, lanes=lanes):
                s, rk = sr
                m = jnp.max(s, axis=0, keepdims=True)
                first = jnp.min(jnp.where(s == m, key_idx, float(n_key)), axis=0, keepdims=True)
                sel = key_idx == first
                t_scr[a, hk:hk + 1, lanes] = m
                return jnp.where(sel, -inf, s), jnp.where(sel, lax.convert_element_type(a, F32), rk)

            _, rk = lax.fori_loop(0, topk, one_round, (s_scr[hk, :, lanes], no_rank))
            rk_scr[hk, :, lanes] = rk

    n_vreg = n_key // SUBLANES
    tied_hk = []
    for hk in range(2 * n_head):
        tied = jnp.zeros((SUBLANES, LANES), F32)
        for lanes in chunks:
            keys = [s_scr[hk, v * SUBLANES:(v + 1) * SUBLANES, lanes] for v in range(n_vreg)]
            top = _sorted_top(keys)
            for a in range(topk):
                t_scr[a, hk:hk + 1, lanes] = top[a][0:1, :]
            if hk < n_head:
                n_ranked = sum(jnp.where(x >= top[topk - 1], 1.0, 0.0) for x in keys)
            else:
                ranks = [_count_greater(top, x) for x in keys]
                rk_scr[hk, :, lanes] = jnp.concatenate(ranks, axis=0)
                n_ranked = sum(jnp.where(r < float(topk), 1.0, 0.0) for r in ranks)
            for step in (4, 2, 1):
                n_ranked = n_ranked + pltpu.roll(n_ranked, step, 0)
            tied = jnp.maximum(tied, jnp.abs(n_ranked - float(topk)))
            for a in range(topk - 1):
                tied = jnp.maximum(tied, jnp.where(top[a] == top[a + 1], 1.0, 0.0))
        tied_hk.append(jnp.max(tied))

    for hk in range(2 * n_head):
        @pl.when(tied_hk[hk] > 0.0)
        def _(hk=hk):
            exact_ranks(hk)

    cands = _peer_candidates(topk)
    for lc in range(n_lc):
        lanes = slice(lc * LANES, (lc + 1) * LANES)
        t0 = [t_scr[a, 0:n_head, lanes] for a in range(topk)]
        t1 = [t_scr[b, n_head:2 * n_head, lanes] for b in range(topk)]
        val = [t0[a] + t1[b] for a, b in cands]
        ordered = lambda p, q: (p[0] <= q[0] and p[1] <= q[1]) or (q[0] <= p[0] and q[1] <= p[1])
        rank = []
        for c1, (a, b) in enumerate(cands):
            later = sum(not ordered(cands[c1], q) for q in cands[c1 + 1:])
            rank.append(jnp.full((n_head, LANES), float((a + 1) * (b + 1) - 1 + later), F32))
        for c1 in range(len(cands)):
            for c2 in range(c1 + 1, len(cands)):
                if ordered(cands[c1], cands[c2]):
                    continue
                ge = jnp.where(val[c1] >= val[c2], 1.0, 0.0)
                rank[c2] = rank[c2] + ge
                rank[c1] = rank[c1] - ge
        sel = [jnp.where(r < float(topk), 1.0, 0.0) for r in rank]
        zsum = jnp.zeros((n_head, LANES), F32)
        cnt = [jnp.zeros((n_head, LANES), F32) for _ in range(topk)]
        col = [jnp.zeros((n_head, LANES), F32) for _ in range(topk)]
        for c, (a, b) in enumerate(cands):
            zsum = zsum + sel[c] * jnp.exp(val[c] - val[0])
            cnt[a] = cnt[a] + sel[c]
            col[b] = col[b] + sel[c]
        zi_scr[:, lanes] = 0.5 / zsum
        for a in range(topk):
            cn_scr[a, :, lanes] = cnt[a]
        for b in range(topk):
            thr = jnp.full((n_head, LANES), inf, F32)
            for a in range(topk):
                thr = jnp.where(col[b] == float(a + 1), t0[a], thr)
            th_scr[b, :, lanes] = thr

    for h in range(n_head):
        @pl.when(tied_hk[h] > 0.0)
        def _(h=h):
            r0 = rk_scr[h]
            c = jnp.zeros((n_key, tm), F32)
            for a in range(topk):
                c = jnp.where(r0 == float(a), cn_scr[a, h:h + 1, :], c)
            cnt_ref[h] = c

        @pl.when(tied_hk[h] <= 0.0)
        def _(h=h):
            for lanes in chunks:
                desc = [jnp.broadcast_to(th_scr[b, h:h + 1, lanes], (SUBLANES, LANES)) for b in reversed(range(topk))]
                for v in range(n_vreg):
                    rows = slice(v * SUBLANES, (v + 1) * SUBLANES)
                    cnt_ref[h, rows, lanes] = float(topk) - _count_greater(desc, s_scr[h, rows, lanes])

        e0_ref[h] = jnp.exp(s_scr[h] - t_scr[0, h:h + 1, :]) * zi_scr[h:h + 1, :]
        e1_ref[h] = jnp.exp(s_scr[n_head + h] - t_scr[0, n_head + h:n_head + h + 1, :]).astype(BF16)
        r1_ref[h] = rk_scr[n_head + h].astype(BF16)


def _router(h2t, wq_t, keys_hk, n_head, tm):
    d, n_tok = h2t.shape
    _, n_key, half = keys_hk.shape
    topk = PEER_TOPK
    out = lambda dt: jax.ShapeDtypeStruct((n_head, n_key, n_tok), dt)
    ospec = pl.BlockSpec((n_head, n_key, tm), lambda i: (0, 0, i))
    return pl.pallas_call(
        functools.partial(_router_body, tm=tm, n_head=n_head, n_key=n_key, half=half, topk=topk),
        grid=(n_tok // tm,),
        in_specs=[pl.BlockSpec((d, tm), lambda i: (0, i)),
                  pl.BlockSpec(wq_t.shape, lambda i: (0, 0)),
                  pl.BlockSpec(keys_hk.shape, lambda i: (0, 0, 0))],
        out_specs=[ospec, ospec, ospec, ospec],
        out_shape=[out(BF16), out(BF16), out(F32), out(F32)],
        scratch_shapes=[pltpu.VMEM((wq_t.shape[0], tm), BF16),
                        pltpu.VMEM((2 * n_head, n_key, tm), F32),
                        pltpu.VMEM((2 * n_head, n_key, tm), F32),
                        pltpu.VMEM((topk, 2 * n_head, tm), F32),
                        pltpu.VMEM((topk, n_head, tm), F32),
                        pltpu.VMEM((topk, n_head, tm), F32),
                        pltpu.VMEM((n_head, tm), F32)],
        compiler_params=_params("parallel"),
        name="router",
    )(h2t, wq_t, keys_hk)


def _peer_body(h2t_ref, u_ref, v_ref, r1_ref, e1_ref, cnt_ref, e0_ref, o_ref, g_scr, *, ni, n_key, n_head, tm):
    e = pl.program_id(1)

    @pl.when(e == 0)
    def _():
        o_ref[...] = jnp.zeros(o_ref.shape, F32)

    base = pl.multiple_of(e * ni, ni)
    cblk = [cnt_ref[h, pl.ds(base, ni), :] for h in range(n_head)]
    eblk = [e0_ref[h, pl.ds(base, ni), :] for h in range(n_head)]
    for ii in range(ni):
        rows = slice(ii * n_key, (ii + 1) * n_key)
        act = jnp.dot(u_ref[rows, :], h2t_ref[...], preferred_element_type=F32)
        cbs = [jnp.broadcast_to(cblk[h][ii:ii + 1, :], (BF16_ROWS, tm)).astype(BF16) for h in range(n_head)]
        ebs = [jnp.broadcast_to(eblk[h][ii:ii + 1, :], (BF16_ROWS, tm)).astype(BF16) for h in range(n_head)]
        tiles = []
        for k0 in range(0, n_key, BF16_ROWS):
            w = None
            for h in range(n_head):
                t = jnp.where(r1_ref[h, k0:k0 + BF16_ROWS, :] < cbs[h], e1_ref[h, k0:k0 + BF16_ROWS, :],
                              jnp.zeros((), BF16)) * ebs[h]
                w = t if w is None else w + t
            tiles.append(w)
        act2 = act * (1.0 + lax.erf(act * (1.0 / math.sqrt(2.0))))
        g_scr[rows, :] = act2.astype(BF16) * jnp.concatenate(tiles, axis=0)
    o_ref[...] += lax.dot_general(g_scr[...], v_ref[...], (((0,), (0,)), ((), ())), preferred_element_type=F32)


def _peer(h2t, u_bf, v_bf, r1, e1, cnt, e0, tm, ni):
    d, n_tok = h2t.shape
    n_head, n_key, _ = r1.shape
    te = ni * n_key
    hspec = pl.BlockSpec((n_head, n_key, tm), lambda t, e: (0, 0, t))
    return pl.pallas_call(
        functools.partial(_peer_body, ni=ni, n_key=n_key, n_head=n_head, tm=tm),
        grid=(n_tok // tm, n_key // ni),
        in_specs=[pl.BlockSpec((d, tm), lambda t, e: (0, t)),
                  pl.BlockSpec((te, d), lambda t, e: (e, 0)),
                  pl.BlockSpec((te, d), lambda t, e: (e, 0)),
                  hspec, hspec, hspec, hspec],
        out_specs=pl.BlockSpec((tm, d), lambda t, e: (t, 0)),
        out_shape=jax.ShapeDtypeStruct((n_tok, d), F32),
        scratch_shapes=[pltpu.VMEM((te, tm), BF16)],
        compiler_params=_params("parallel", "arbitrary"),
        name="peer",
    )(h2t, u_bf, v_bf, r1, e1, cnt, e0)


def _final_body(x1_ref, p_ref, gt_ref, gf_ref, o_ref, *, last):
    x2 = x1_ref[...] + _mod_rows(gt_ref, x1_ref.shape[0]) * p_ref[...]
    o_ref[...] = _rms(x2, gf_ref[...]) if last else x2


def _final(x1, peer_out, row_off, gt, g_final, tm, tiles_per_group, last):
    n_tok, d = x1.shape
    off = row_off // tm
    return pl.pallas_call(
        functools.partial(_final_body, last=last),
        grid=(n_tok // tm,),
        in_specs=[pl.BlockSpec((tm, d), lambda i: (i, 0)), pl.BlockSpec((tm, d), lambda i: (i + off, 0)),
                  _mod_spec(gt, tiles_per_group), pl.BlockSpec((1, d), lambda i: (0, 0))],
        out_specs=pl.BlockSpec((tm, d), lambda i: (i, 0)),
        out_shape=jax.ShapeDtypeStruct((n_tok, d), F32),
        compiler_params=_params("parallel"),
        name="final",
    )(x1, peer_out, gt, g_final.reshape(1, d))


def kernel(x_prompt, x_sample, state_s5_re, state_s5_im, state_pool, c_prompt, c_sample, w_ada, b_ada, g_norm1, g_norm2, w_in, s5_a_re, s5_a_im, s5_log_dt, s5_b_re, s5_b_im, s5_c_re, s5_c_im, s5_d, w_glu, b_glu, w_pool, pool_scale, w_proj_a, w_proj_b, w_out, peer_wq, peer_keys, peer_u, peer_v, g_final):
    depth = w_ada.shape[0]
    nb, nt, d = x_prompt.shape
    sb, st, _ = x_sample.shape
    n_grp, n_state, grp_w = s5_b_re.shape[1:]
    s5_w = n_grp * grp_w
    n_head = peer_keys.shape[1]
    n_key = peer_keys.shape[3]
    pool_buf = state_pool.shape[2]
    n_p, n_s = nb * nt, sb * st
    n_chunk = nt // S5_CHUNK
    n_lvl = int(math.log2(n_chunk))
    assert 2 ** n_lvl == n_chunk and st <= S5_CHUNK and pool_buf == max(POOL_WINDOWS) - 1

    tm_a, tm_b, tm_r, tm_e = 512, 512, 512, 512
    assert nt % tm_a == 0 and n_s % tm_a == 0 and (n_p + n_s) % tm_e == 0

    xp = x_prompt.reshape(n_p, d)
    xs = jnp.swapaxes(x_sample, 0, 1).reshape(n_s, d)
    c_all = jnp.concatenate([c_prompt, c_sample], axis=0)
    c_all = jnp.pad(c_all, ((0, (-c_all.shape[0]) % 8), (0, 0)))

    new_p = {"re": [], "im": [], "buf": []}
    new_s = {"re": [], "im": [], "buf": []}
    for l in range(depth):
        last = l == depth - 1
        mod = _ada(c_all, w_ada[l], b_ada[l])
        mods_p = [m.reshape(nb, 1, d) for m in jnp.split(mod[:nb], 6, axis=-1)]
        mods_s = [m.reshape(1, sb, d) for m in jnp.split(mod[nb:nb + sb], 6, axis=-1)]

        w_in_bf = w_in[l].astype(BF16)
        tm_in = min(nt, 2 * tm_a)
        u_p, gates_p = _in_proj(xp, mods_p[1], mods_p[0], g_norm1[l], w_in_bf, 2 * s5_w, tm_in, nt // tm_in)
        u_s, gates_s = _in_proj(xs, mods_s[1], mods_s[0], g_norm1[l], w_in_bf, 2 * s5_w, tm_a, n_s // tm_a)

        k_out, e_re, e_im, f_re, f_im, pw_re, pw_im = _s5_prep(
            s5_a_re[l], s5_a_im[l], s5_log_dt[l], s5_b_re[l], s5_b_im[l], s5_c_re[l], s5_c_im[l],
            S5_CHUNK, n_lvl, st)
        ar = jnp.concatenate([pw_re, pw_re], axis=-1)
        ai = jnp.concatenate([-pw_im, pw_im], axis=-1)
        m16, si16, so16 = _s5_matrices(k_out, e_re, e_im, f_re, f_im, S5_CHUNK)
        m4, si4, so4 = _s5_matrices(k_out, e_re, e_im, f_re, f_im, st)

        tr = lambda a: jnp.swapaxes(a, 1, 2)
        z_p, hf_p = _s5_prompt(u_p, tr(m16), tr(si16), tr(so16), ar, ai, s5_d[l], nb, S5_CHUNK, grp_w)
        hf_p = hf_p.transpose(1, 0, 2)
        new_p["re"].append(hf_p[..., :n_state])
        new_p["im"].append(hf_p[..., n_state:])

        h0 = jnp.concatenate([state_s5_re[l], state_s5_im[l]], axis=-1).transpose(1, 0, 2)
        z_s, hn_s = _s5_sample(u_s, h0, m4, si4, so4, ar, ai, s5_d[l], n_lvl, st, grp_w)
        hn_s = hn_s.transpose(1, 0, 2)
        new_s["re"].append(hn_s[..., :n_state])
        new_s["im"].append(hn_s[..., n_state:])

        new_p["buf"].append(u_p.reshape(nb, nt, 2 * s5_w)[:, nt - pool_buf:, s5_w:])
        hist = jnp.swapaxes(state_pool[l], 0, 1)
        ub_s = u_s[:, s5_w:].reshape(st, sb, s5_w)
        new_s["buf"].append(jnp.swapaxes(jnp.concatenate([hist, ub_s], axis=0)[st:], 0, 1))
        halo_s = hist.reshape(pool_buf * sb, s5_w)

        wb = lambda a: a.astype(BF16)
        mix_w = (wb(w_glu[l]), b_glu[l], wb(w_pool[l]), pool_scale[l], wb(w_proj_a[l]), wb(w_proj_b[l]))
        merged_p = _mixer(z_p, u_p, None, gates_p, *mix_w, tm=tm_b, stride=1, tiles_per_seq=nt // tm_b,
                          count_positions=True)
        merged_s = _mixer(z_s, u_s, halo_s, gates_s, *mix_w, tm=n_s, stride=sb, tiles_per_seq=1,
                          count_positions=False)

        w_o_bf = wb(w_out[l])
        x1_p, h2t_p = _out_proj(merged_p, xp, mods_p[2], mods_p[4], mods_p[3], g_norm2[l], w_o_bf, tm_b, nt // tm_b)
        x1_s, h2t_s = _out_proj(merged_s, xs, mods_s[2], mods_s[4], mods_s[3], g_norm2[l], w_o_bf, tm_b, n_s // tm_b)

        h2t = jnp.concatenate([h2t_p, h2t_s], axis=1)
        keys_hk = wb(jnp.swapaxes(peer_keys[l], 0, 1).reshape(2 * n_head, n_key, -1))
        r1, e1, cnt, e0 = _router(h2t, wb(peer_wq[l].T), keys_hk, n_head, tm_r)
        peer_out = _peer(h2t, wb(peer_u[l]), wb(peer_v[l]), r1, e1, cnt, e0, tm_e, 8)

        xp = _final(x1_p, peer_out, 0, mods_p[5], g_final, tm_a, nt // tm_a, last)
        xs = _final(x1_s, peer_out, n_p, mods_s[5], g_final, tm_a, n_s // tm_a, last)

    y_prompt = xp.reshape(nb, nt, d)
    y_sample = jnp.swapaxes(xs.reshape(st, sb, d), 0, 1)
    stack = lambda xs_: jnp.stack(xs_)
    return (y_prompt, y_sample, stack(new_p["re"]), stack(new_p["im"]), stack(new_p["buf"]),
            stack(new_s["re"]), stack(new_s["im"]), stack(new_s["buf"]))
```
